```python
import math
import jax, jax.numpy as jnp
from jax import lax
import numpy as np

D_MODEL = 1024
BATCH = 2
SEQ = 16384
DEPTH = 4

N_MIXERS = 3
N_MLSTM = (DEPTH + 2) // 3
N_S5 = (DEPTH + 1) // 3
N_SWA = DEPTH // 3
DN_ALPHA = (2 * DEPTH) ** 0.25
DN_BETA = (8 * DEPTH) ** -0.25
LN_EPS = 1e-5

M_HEADS = 4
M_DQK = D_MODEL // 8
M_DV = D_MODEL // M_HEADS
M_CONV = 4
M_CHUNK = 64
M_IN = 2 * M_HEADS * M_DQK + M_HEADS * M_DV + D_MODEL + 2 * M_HEADS

S5_GROUP = 16
S5_GROUPS = D_MODEL // S5_GROUP
S5_STATE = 64
S5_CHUNK = 128

A_HEADS = 16
A_KV_HEADS = 4
A_GROUP = A_HEADS // A_KV_HEADS
A_HEAD_DIM = D_MODEL // A_HEADS
WINDOW = 128
A_BLOCK = 128

E_GROUPS = 4
E_PER_GROUP = 8
N_EXPERTS = E_GROUPS * E_PER_GROUP
TOP_K = 2
D_EXPERT = 512
MOE_BLOCK = 128

kernel_name = 'hybrid_mlstm_s5_swa_hmoe_deepnorm'


def _layer_norm(x, g, b):
    xf = x.astype(jnp.float32)
    mu = xf.mean(-1, keepdims=True)
    var = jnp.mean(jnp.square(xf - mu), -1, keepdims=True)
    return ((xf - mu) * lax.rsqrt(var + LN_EPS)).astype(x.dtype) * g + b


def _causal_dwconv(x, w, b):
    k = w.shape[0]
    y = lax.conv_general_dilated(x, w[:, None, :], window_strides=(1,), padding=[(k - 1, 0)],
                                 dimension_numbers=('NWC', 'WIO', 'NWC'), feature_group_count=x.shape[-1])
    return y + b


def _mlstm_chunk_step(carry, inp):
    C, n, m = carry
    q, k, v, li, lf = inp
    L = q.shape[2]
    causal = jnp.tril(jnp.ones((L, L), bool))
    b = jnp.cumsum(lf, axis=-1)
    d = jnp.where(causal, b[..., :, None] - b[..., None, :] + li[..., None, :], -jnp.inf)
    inter = b + m[..., None]
    m_t = jnp.maximum(inter, d.max(-1))
    w = jnp.exp(d - m_t[..., None]) * jnp.einsum('bhtk,bhsk->bhts', q, k)
    e_inter = jnp.exp(inter - m_t)
    num = jnp.einsum('bhts,bhsv->bhtv', w, v) + e_inter[..., None] * jnp.einsum('bhtk,bhkv->bhtv', q, C)
    den = w.sum(-1) + e_inter * jnp.einsum('bhtk,bhk->bht', q, n)
    h = num / jnp.maximum(jnp.abs(den), jnp.exp(-m_t))[..., None]
    b_last = b[..., -1]
    g = b_last[..., None] - b + li
    m_new = jnp.maximum(b_last + m, g.max(-1))
    decay = jnp.exp(b_last + m - m_new)
    wk = jnp.exp(g - m_new[..., None])[..., None] * k
    C_new = decay[..., None, None] * C + jnp.einsum('bhsk,bhsv->bhkv', wk, v)
    n_new = decay[..., None] * n + wk.sum(2)
    return (C_new, n_new, m_new), h


def _mlstm_mixer(h, w_in, conv_w, conv_b, b_if, norm_g, w_out):
    Bn, S, _ = h.shape
    nc = S // M_CHUNK
    f32 = jnp.float32
    n_qk = 2 * M_HEADS * M_DQK
    n_v = M_HEADS * M_DV
    proj = h @ w_in
    qk = jax.nn.silu(_causal_dwconv(proj[..., :n_qk], conv_w, conv_b))
    v = proj[..., n_qk:n_qk + n_v]
    o = proj[..., n_qk + n_v:n_qk + n_v + D_MODEL]
    gates = (proj[..., n_qk + n_v + D_MODEL:] + b_if).astype(f32)
    li = gates[..., :M_HEADS]
    lf = jax.nn.log_sigmoid(gates[..., M_HEADS:])

    def to_chunks(t, dh):
        return t.astype(f32).reshape(Bn, nc, M_CHUNK, M_HEADS, dh).transpose(1, 0, 3, 2, 4)

    def gate_chunks(t):
        return t.reshape(Bn, nc, M_CHUNK, M_HEADS).transpose(1, 0, 3, 2)

    q = to_chunks(qk[..., :M_HEADS * M_DQK], M_DQK)
    k = to_chunks(qk[..., M_HEADS * M_DQK:], M_DQK) * (M_DQK ** -0.5)
    vc = to_chunks(v, M_DV)
    init = (jnp.zeros((Bn, M_HEADS, M_DQK, M_DV), f32), jnp.zeros((Bn, M_HEADS, M_DQK), f32),
            jnp.zeros((Bn, M_HEADS), f32))
    _, hs = lax.scan(_mlstm_chunk_step, init, (q, k, vc, gate_chunks(li), gate_chunks(lf)))
    hs = hs.transpose(1, 0, 3, 2, 4).reshape(Bn, S, M_HEADS, M_DV)
    mu = hs.mean(-1, keepdims=True)
    var = jnp.mean(jnp.square(hs - mu), -1, keepdims=True)
    hn = ((hs - mu) * lax.rsqrt(var + LN_EPS)).reshape(Bn, S, n_v).astype(h.dtype) * norm_g
    return (hn * jax.nn.sigmoid(o)) @ w_out


def _cmul_scan_op(e1, e2):
    a1r, a1i, b1r, b1i = e1
    a2r, a2i, b2r, b2i = e2
    return (a2r * a1r - a2i * a1i, a2r * a1i + a2i * a1r,
            a2r * b1r - a2i * b1i + b2r, a2r * b1i + a2i * b1r + b2i)


def _s5_mixer(h, w_in, lam_re, lam_im, log_dt, b_re, b_im, c_re, c_im, d_skip, w_glu):
    Bn, S, _ = h.shape
    f32 = jnp.float32
    L = S5_CHUNK
    nc = S // L
    u = h @ w_in
    lr = lam_re.astype(f32)
    lim = lam_im.astype(f32)
    dt = jnp.exp(log_dt.astype(f32))[:, None]
    mag = jnp.exp(lr * dt)
    abar_re = mag * jnp.cos(lim * dt)
    abar_im = mag * jnp.sin(lim * dt)
    lam_sq = lr * lr + lim * lim
    t_re = ((abar_re - 1.0) * lr + abar_im * lim) / lam_sq
    t_im = (abar_im * lr - (abar_re - 1.0) * lim) / lam_sq
    br = b_re.astype(f32)
    bi = b_im.astype(f32)
    bb_re = t_re[..., None] * br - t_im[..., None] * bi
    bb_im = t_re[..., None] * bi + t_im[..., None] * br
    steps = jnp.arange(1, L + 1, dtype=f32)[:, None, None]
    pmag = jnp.exp(lr * dt * steps)
    pw_re = pmag * jnp.cos(lim * dt * steps)
    pw_im = pmag * jnp.sin(lim * dt * steps)
    cr = c_re.astype(f32)
    ci = c_im.astype(f32)

    def step(carry, u_c):
        x_re, x_im = carry
        bu_re = jnp.einsum('blgi,gpi->blgp', u_c, bb_re)
        bu_im = jnp.einsum('blgi,gpi->blgp', u_c, bb_im)
        a_re = jnp.broadcast_to(abar_re, bu_re.shape)
        a_im = jnp.broadcast_to(abar_im, bu_re.shape)
        _, _, xs_re, xs_im = lax.associative_scan(_cmul_scan_op, (a_re, a_im, bu_re, bu_im), axis=1)
        xs_re = xs_re + pw_re * x_re[:, None] - pw_im * x_im[:, None]
        xs_im = xs_im + pw_re * x_im[:, None] + pw_im * x_re[:, None]
        y = jnp.einsum('blgp,gip->blgi', xs_re, cr) - jnp.einsum('blgp,gip->blgi', xs_im, ci)
        return (xs_re[:, -1], xs_im[:, -1]), y

    uc = u.astype(f32).reshape(Bn, nc, L, S5_GROUPS, S5_GROUP).transpose(1, 0, 2, 3, 4)
    init = (jnp.zeros((Bn, S5_GROUPS, S5_STATE), f32), jnp.zeros((Bn, S5_GROUPS, S5_STATE), f32))
    _, ys = lax.scan(step, init, uc)
    y = ys.transpose(1, 0, 2, 3, 4).reshape(Bn, S, D_MODEL).astype(h.dtype) + d_skip * u
    z = jax.nn.gelu(y) @ w_glu
    return z[..., :D_MODEL] * jax.nn.sigmoid(z[..., D_MODEL:])


def _swa_mixer(h, w_qkv, b_qkv, sinks, w_o):
    Bn, S, _ = h.shape
    nb = S // A_BLOCK
    f32 = jnp.float32
    qkv = h @ w_qkv + b_qkv
    nq = A_HEADS * A_HEAD_DIM
    nkv = A_KV_HEADS * A_HEAD_DIM
    q = qkv[..., :nq].reshape(Bn, nb, A_BLOCK, A_KV_HEADS, A_GROUP, A_HEAD_DIM)
    k = qkv[..., nq:nq + nkv].reshape(Bn, S, A_KV_HEADS, A_HEAD_DIM)
    v = qkv[..., nq + nkv:].reshape(Bn, S, A_KV_HEADS, A_HEAD_DIM)

    def band(t):
        tp = jnp.pad(t, ((0, 0), (A_BLOCK, 0), (0, 0), (0, 0))).reshape(Bn, nb + 1, A_BLOCK, A_KV_HEADS, A_HEAD_DIM)
        return jnp.concatenate([tp[:, :-1], tp[:, 1:]], axis=2)

    kw = band(k)
    vw = band(v)
    s = jnp.einsum('bnqhgd,bnkhd->bhgnqk', q.astype(f32), kw.astype(f32)) * (A_HEAD_DIM ** -0.5)
    blk = jnp.arange(nb)[:, None, None] * A_BLOCK
    qpos = blk + jnp.arange(A_BLOCK)[None, :, None]
    kpos = blk - A_BLOCK + jnp.arange(2 * A_BLOCK)[None, None, :]
    valid = (kpos <= qpos) & (qpos - kpos < WINDOW) & (kpos >= 0)
    s = jnp.where(valid, s, -jnp.inf)
    sink = sinks.astype(f32).reshape(A_KV_HEADS, A_GROUP)[None, :, :, None, None]
    m = jnp.maximum(s.max(-1), sink)
    p = jnp.exp(s - m[..., None])
    p = p / (p.sum(-1) + jnp.exp(sink - m))[..., None]
    o = jnp.einsum('bhgnqk,bnkhd->bnqhgd', p.astype(h.dtype), vw).reshape(Bn, S, D_MODEL)
    return o @ w_o


def _hier_moe(h, w_group, b_group, w_router, b_router, w1, w3, w2):
    Bn, S, Dm = h.shape
    T = Bn * S
    f32 = jnp.float32
    xt = h.reshape(T, Dm)
    glog = (xt @ w_group + b_group).astype(f32)
    g_sel = jnp.argmax(glog, -1)
    p_g = jnp.take_along_axis(jax.nn.softmax(glog, -1), g_sel[:, None], -1)[:, 0]
    elog = (xt @ w_router + b_router).astype(f32).reshape(T, E_GROUPS, E_PER_GROUP)
    elog_g = jnp.take_along_axis(elog, g_sel[:, None, None], 1)[:, 0]
    top_v, top_i = lax.top_k(elog_g, TOP_K)
    w_tok = jax.nn.softmax(top_v, -1) * p_g[:, None]
    e_a = (g_sel[:, None] * E_PER_GROUP + top_i).reshape(-1)
    tok_a = jnp.repeat(jnp.arange(T, dtype=jnp.int32), TOP_K)
    w_a = w_tok.reshape(-1)
    n_assign = T * TOP_K
    order = jnp.argsort(e_a)
    se = e_a[order]
    stok = tok_a[order]
    sw = w_a[order]
    counts = jax.ops.segment_sum(jnp.ones_like(e_a), e_a, num_segments=N_EXPERTS)
    padded = (counts + MOE_BLOCK - 1) // MOE_BLOCK * MOE_BLOCK
    start = jnp.cumsum(counts) - counts
    pend = jnp.cumsum(padded)
    pstart = pend - padded
    pos = pstart[se] + (jnp.arange(n_assign) - start[se])
    P = n_assign + N_EXPERTS * MOE_BLOCK
    nblk = P // MOE_BLOCK
    buf_tok = jnp.zeros((P,), jnp.int32).at[pos].set(stok)
    buf_w = jnp.zeros((P,), f32).at[pos].set(sw)
    buf_x = jnp.zeros((P, Dm), h.dtype).at[pos].set(xt[stok])
    blk_e = jnp.clip(jnp.searchsorted(pend, jnp.arange(nblk) * MOE_BLOCK, side='right'), 0, N_EXPERTS - 1)

    def expert_block(args):
        xb, e = args
        return (jax.nn.silu(xb @ w1[e]) * (xb @ w3[e])) @ w2[e]

    y = lax.map(expert_block, (buf_x.reshape(nblk, MOE_BLOCK, Dm), blk_e)).reshape(P, Dm)
    y = y * buf_w[:, None].astype(y.dtype)
    return jnp.zeros((T, Dm), h.dtype).at[buf_tok].add(y).reshape(Bn, S, Dm)


def setup_inputs(seed: int = 0) -> dict:
    key = jax.random.key(seed)
    ks = iter(jax.random.split(key, 64))
    f32 = jnp.float32
    D = D_MODEL

    def nrm(shape, std):
        return jax.random.normal(next(ks), shape, f32) * std

    inp = {}
    inp['x'] = nrm((BATCH, SEQ, D), 1.0)
    inp['c'] = nrm((BATCH, D), 1.0)
    inp['ada_w'] = nrm((DEPTH, D, 6 * D), 0.5 * D ** -0.5)
    inp['ada_b'] = nrm((DEPTH, 6 * D), 0.01)
    inp['ln1_g'] = 1.0 + nrm((DEPTH, D), 0.02)
    inp['ln1_b'] = nrm((DEPTH, D), 0.01)
    inp['ln2_g'] = 1.0 + nrm((DEPTH, D), 0.02)
    inp['ln2_b'] = nrm((DEPTH, D), 0.01)
    inp['mlstm_w_in'] = nrm((N_MLSTM, D, M_IN), D ** -0.5)
    inp['mlstm_conv_w'] = nrm((N_MLSTM, M_CONV, 2 * M_HEADS * M_DQK), M_CONV ** -0.5)
    inp['mlstm_conv_b'] = nrm((N_MLSTM, 2 * M_HEADS * M_DQK), 0.01)
    f_bias = 3.0 + 3.0 * jax.random.uniform(next(ks), (N_MLSTM, M_HEADS), f32)
    inp['mlstm_b_if'] = jnp.concatenate([nrm((N_MLSTM, M_HEADS), 0.1) - 1.0, f_bias], -1)
    inp['mlstm_norm_g'] = 1.0 + nrm((N_MLSTM, M_HEADS * M_DV), 0.02)
    inp['mlstm_w_out'] = nrm((N_MLSTM, M_HEADS * M_DV, D), (M_HEADS * M_DV) ** -0.5 * DN_BETA)
    inp['s5_w_in'] = nrm((N_S5, D, D), D ** -0.5)
    inp['s5_lam_re'] = -0.5 + nrm((N_S5, S5_GROUPS, S5_STATE), 0.01)
    inp['s5_lam_im'] = math.pi * jnp.arange(S5_STATE, dtype=f32)[None, None, :] + nrm((N_S5, S5_GROUPS, S5_STATE), 0.01)
    inp['s5_log_dt'] = jax.random.uniform(next(ks), (N_S5, S5_GROUPS), f32, minval=math.log(1e-3), maxval=math.log(1e-1))
    inp['s5_b_re'] = nrm((N_S5, S5_GROUPS, S5_STATE, S5_GROUP), (2 * S5_GROUP) ** -0.5)
    inp['s5_b_im'] = nrm((N_S5, S5_GROUPS, S5_STATE, S5_GROUP), (2 * S5_GROUP) ** -0.5)
    inp['s5_c_re'] = nrm((N_S5, S5_GROUPS, S5_GROUP, S5_STATE), S5_STATE ** -0.5)
    inp['s5_c_im'] = nrm((N_S5, S5_GROUPS, S5_GROUP, S5_STATE), S5_STATE ** -0.5)
    inp['s5_d'] = nrm((N_S5, D), 1.0)
    inp['s5_w_glu'] = jnp.concatenate([nrm((N_S5, D, D), D ** -0.5 * DN_BETA), nrm((N_S5, D, D), D ** -0.5)], -1)
    n_qkv = (A_HEADS + 2 * A_KV_HEADS) * A_HEAD_DIM
    inp['swa_w_qkv'] = nrm((N_SWA, D, n_qkv), D ** -0.5)
    inp['swa_b_qkv'] = nrm((N_SWA, n_qkv), 0.01)
    inp['swa_sinks'] = nrm((N_SWA, A_HEADS), 0.5)
    inp['swa_w_o'] = nrm((N_SWA, A_HEADS * A_HEAD_DIM, D), (A_HEADS * A_HEAD_DIM) ** -0.5 * DN_BETA)
    inp['moe_w_group'] = nrm((DEPTH, D, E_GROUPS), D ** -0.5)
    inp['moe_b_group'] = nrm((DEPTH, E_GROUPS), 0.01)
    inp['moe_w_router'] = nrm((DEPTH, D, N_EXPERTS), D ** -0.5)
    inp['moe_b_router'] = nrm((DEPTH, N_EXPERTS), 0.01)
    inp['moe_w1'] = nrm((DEPTH, N_EXPERTS, D, D_EXPERT), D ** -0.5)
    inp['moe_w3'] = nrm((DEPTH, N_EXPERTS, D, D_EXPERT), D ** -0.5)
    inp['moe_w2'] = nrm((DEPTH, N_EXPERTS, D_EXPERT, D), D_EXPERT ** -0.5 * DN_BETA)
    return inp


def reference(x, c, ada_w, ada_b, ln1_g, ln1_b, ln2_g, ln2_b,
              mlstm_w_in, mlstm_conv_w, mlstm_conv_b, mlstm_b_if, mlstm_norm_g, mlstm_w_out,
              s5_w_in, s5_lam_re, s5_lam_im, s5_log_dt, s5_b_re, s5_b_im, s5_c_re, s5_c_im, s5_d, s5_w_glu,
              swa_w_qkv, swa_b_qkv, swa_sinks, swa_w_o,
              moe_w_group, moe_b_group, moe_w_router, moe_b_router, moe_w1, moe_w3, moe_w2):
    cond = jax.nn.silu(c)
    for i in range(DEPTH):
        mod = cond @ ada_w[i] + ada_b[i]
        sh1, sc1, g1, sh2, sc2, g2 = jnp.split(mod, 6, axis=-1)
        hin = x * (1.0 + sc1[:, None]) + sh1[:, None]
        kind = i % N_MIXERS
        j = i // N_MIXERS
        if kind == 0:
            y = _mlstm_mixer(hin, mlstm_w_in[j], mlstm_conv_w[j], mlstm_conv_b[j], mlstm_b_if[j],
                             mlstm_norm_g[j], mlstm_w_out[j])
        elif kind == 1:
            y = _s5_mixer(hin, s5_w_in[j], s5_lam_re[j], s5_lam_im[j], s5_log_dt[j], s5_b_re[j], s5_b_im[j],
                          s5_c_re[j], s5_c_im[j], s5_d[j], s5_w_glu[j])
        else:
            y = _swa_mixer(hin, swa_w_qkv[j], swa_b_qkv[j], swa_sinks[j], swa_w_o[j])
        x = _layer_norm(DN_ALPHA * x + g1[:, None] * y, ln1_g[i], ln1_b[i])
        hin = x * (1.0 + sc2[:, None]) + sh2[:, None]
        y = _hier_moe(hin, moe_w_group[i], moe_b_group[i], moe_w_router[i], moe_b_router[i],
                      moe_w1[i], moe_w3[i], moe_w2[i])
        x = _layer_norm(DN_ALPHA * x + g2[:, None] * y, ln2_g[i], ln2_b[i])
    return x
```

```python
import functools
import math

import jax
import jax.numpy as jnp
from jax import lax
from jax.experimental import pallas as pl
from jax.experimental.pallas import tpu as pltpu

F32 = jnp.float32
BF16 = jnp.bfloat16
I32 = jnp.int32

D_MODEL = 1024
DEPTH = 4
N_MIXERS = 3
DN_ALPHA = (2 * DEPTH) ** 0.25
LN_EPS = 1e-5

M_HEADS = 4
M_DQK = D_MODEL // 8
M_DV = D_MODEL // M_HEADS
M_CONV = 4
N_QK = 2 * M_HEADS * M_DQK
N_V = M_HEADS * M_DV

S5_GROUP = 16
S5_GROUPS = D_MODEL // S5_GROUP
S5_STATE = 64

A_HEADS = 16
A_KV_HEADS = 4
A_GROUP = A_HEADS // A_KV_HEADS
A_HEAD_DIM = D_MODEL // A_HEADS
WINDOW = 128

E_GROUPS = 4
E_PER_GROUP = 8
N_EXPERTS = E_GROUPS * E_PER_GROUP
TOP_K = 2
D_EXPERT = 512

LANES = 128
SUBLANES = 8
CHUNK = LANES
MOE_BM = 256
VMEM_LIMIT = 56 * 1024 * 1024


def _cparams(n_axes):
    return pltpu.CompilerParams(dimension_semantics=("arbitrary",) * n_axes, vmem_limit_bytes=VMEM_LIMIT)


def _split_bf16(a):
    hi = a.astype(BF16)
    lo = (a - hi.astype(F32)).astype(BF16)
    return hi, lo


def _dot3(a, b, dims=None):
    ah, al = _split_bf16(a)
    bh, bl = _split_bf16(b)
    if dims is None:
        dot = lambda x, y: jnp.dot(x, y, preferred_element_type=F32)
    else:
        dot = lambda x, y: lax.dot_general(x, y, dims, preferred_element_type=F32)
    return dot(ah, bh) + (dot(ah, bl) + dot(al, bh))


def _bdot(a, b):
    return jnp.dot(a.astype(BF16), b.astype(BF16), preferred_element_type=F32)


_NT = (((1,), (1,)), ((), ()))
_TN = (((0,), (0,)), ((), ()))


def _layer_norm(v, g, b):
    mu = jnp.mean(v, axis=-1, keepdims=True)
    d = v - mu
    var = jnp.mean(d * d, axis=-1, keepdims=True)
    return d * lax.rsqrt(var + LN_EPS) * g + b


def _sigmoid(x):
    return 1.0 / (1.0 + jnp.exp(-x))


def _silu(x):
    return x * _sigmoid(x)


def _log_sigmoid(x):
    return jnp.minimum(x, 0.0) - jnp.log1p(jnp.exp(-jnp.abs(x)))


def _gelu_tanh(x):
    return 0.5 * x * (1.0 + jnp.tanh(math.sqrt(2.0 / math.pi) * (x + 0.044715 * (x * x * x))))


def _ada_kernel(c_ref, w_ref, b_ref, o_ref):
    cond = _silu(c_ref[...])
    o_ref[...] = _dot3(cond, w_ref[...]) + b_ref[...]


def _ada(c, ada_w, ada_b):
    bn, d = c.shape
    depth, _, n = ada_w.shape
    tn = 1536
    c8 = jnp.zeros((SUBLANES, d), F32).at[:bn].set(c)
    out = pl.pallas_call(
        _ada_kernel,
        grid=(depth, n // tn),
        in_specs=[
            pl.BlockSpec((SUBLANES, d), lambda l, j: (0, 0)),
            pl.BlockSpec((None, d, tn), lambda l, j: (l, 0, j)),
            pl.BlockSpec((None, 1, tn), lambda l, j: (l, 0, j)),
        ],
        out_specs=pl.BlockSpec((None, SUBLANES, tn), lambda l, j: (l, 0, j)),
        out_shape=jax.ShapeDtypeStruct((depth, SUBLANES, n), F32),
        compiler_params=_cparams(2),
        name="ada",
    )(c8, ada_w, ada_b.reshape(depth, 1, n))
    return out[:, :bn]


def _proj_kernel(precise, x_ref, sc_ref, sh_ref, *refs):
    n = len(precise)
    w_refs, b_refs, o_refs = refs[:n], refs[n:2 * n], refs[2 * n:]
    h = x_ref[...] * (1.0 + sc_ref[...]) + sh_ref[...]
    hb = h.astype(BF16)
    for w_ref, b_ref, o_ref, p in zip(w_refs, b_refs, o_refs, precise):
        if p:
            r = _dot3(h, w_ref[...])
        else:
            r = jnp.dot(hb, w_ref[...], preferred_element_type=F32)
        o_ref[...] = r + b_ref[...]


def _proj(x, sc, sh, ws, bs, precise, seq, tm=512):
    t, d = x.shape
    tpb = seq // tm
    vec = pl.BlockSpec((None, 1, d), lambda i: (i // tpb, 0, 0))
    in_specs = [pl.BlockSpec((tm, d), lambda i: (i, 0)), vec, vec]
    in_specs += [pl.BlockSpec(w.shape, lambda i: (0, 0)) for w in ws]
    in_specs += [pl.BlockSpec(b.shape, lambda i: (0, 0)) for b in bs]
    return pl.pallas_call(
        functools.partial(_proj_kernel, tuple(precise)),
        grid=(t // tm,),
        in_specs=in_specs,
        out_specs=[pl.BlockSpec((tm, w.shape[1]), lambda i: (i, 0)) for w in ws],
        out_shape=[jax.ShapeDtypeStruct((t, w.shape[1]), F32) for w in ws],
        compiler_params=_cparams(1),
        name="proj",
    )(x, sc, sh, *ws, *bs)


def _lin_ln_kernel(a_ref, x_ref, g_ref, w_ref, lng_ref, lnb_ref, o_ref):
    y = jnp.dot(a_ref[...].astype(BF16), w_ref[...], preferred_element_type=F32)
    o_ref[...] = _layer_norm(DN_ALPHA * x_ref[...] + g_ref[...] * y, lng_ref[...], lnb_ref[...])


def _lin_ln(a, x, gate, w, lng, lnb, seq, tm=512):
    t, d = x.shape
    k = a.shape[1]
    tpb = seq // tm
    row = pl.BlockSpec((1, d), lambda i: (0, 0))
    return pl.pallas_call(
        _lin_ln_kernel,
        grid=(t // tm,),
        in_specs=[
            pl.BlockSpec((tm, k), lambda i: (i, 0)),
            pl.BlockSpec((tm, d), lambda i: (i, 0)),
            pl.BlockSpec((None, 1, d), lambda i: (i // tpb, 0, 0)),
            pl.BlockSpec((k, d), lambda i: (0, 0)),
            row, row,
        ],
        out_specs=pl.BlockSpec((tm, d), lambda i: (i, 0)),
        out_shape=jax.ShapeDtypeStruct((t, d), F32),
        compiler_params=_cparams(1),
        name="lin_ln",
    )(a, x, gate, w, lng.reshape(1, d), lnb.reshape(1, d))


def _mlstm_kernel(qk_ref, v_ref, o_ref, gates_ref, cw_ref, cb_ref, ng_ref, out_ref,
                  qkbuf, c_ref, m_ref):
    L = CHUNK
    blk = pl.program_id(1)

    @pl.when(blk == 0)
    def _():
        qkbuf[0:SUBLANES, :] = jnp.zeros((SUBLANES, N_QK), F32)
        c_ref[...] = jnp.zeros_like(c_ref)
        m_ref[...] = jnp.zeros_like(m_ref)

    qkbuf[SUBLANES:SUBLANES + L, :] = qk_ref[...]
    conv = cb_ref[...] + cw_ref[0:1, :] * qkbuf[pl.ds(SUBLANES - 3, L), :]
    for j in range(1, M_CONV):
        conv = conv + cw_ref[j:j + 1, :] * qkbuf[pl.ds(SUBLANES - 3 + j, L), :]
    qk = _silu(conv)
    qkbuf[0:SUBLANES, :] = qkbuf[L:L + SUBLANES, :]

    gates = gates_ref[...]
    gates_t = gates.T
    lf_col = _log_sigmoid(gates)
    lf_row = _log_sigmoid(gates_t[0:2 * SUBLANES, :])
    li_row = gates_t[0:M_HEADS, :]
    row_i = lax.broadcasted_iota(I32, (L, L), 0)
    col_i = lax.broadcasted_iota(I32, (L, L), 1)
    causal = col_i <= row_i
    lower = causal.astype(BF16)
    upper = (row_i <= col_i).astype(BF16)
    lf_ch, lf_cl = _split_bf16(lf_col)
    b_col_all = (jnp.dot(lower, lf_ch, preferred_element_type=F32)
                 + jnp.dot(lower, lf_cl, preferred_element_type=F32))
    lf_rh, lf_rl = _split_bf16(lf_row)
    b_row_all = (jnp.dot(lf_rh, upper, preferred_element_type=F32)
                 + jnp.dot(lf_rl, upper, preferred_element_type=F32))
    ones_col = (lax.broadcasted_iota(I32, (L, LANES), 1) == 0).astype(F32)

    for h in range(M_HEADS):
        q = qk[:, h * M_DQK:(h + 1) * M_DQK].astype(BF16)
        k = qk[:, N_QK // 2 + h * M_DQK:N_QK // 2 + (h + 1) * M_DQK] * (M_DQK ** -0.5)
        vaug = jnp.concatenate([v_ref[:, h * M_DV:(h + 1) * M_DV], ones_col], axis=1).astype(BF16)
        b_col = b_col_all[:, M_HEADS + h:M_HEADS + h + 1]
        li_col = gates[:, h:h + 1]
        b_row = b_row_all[M_HEADS + h:M_HEADS + h + 1, :]
        li_r = li_row[h:h + 1, :]
        m_prev = m_ref[h]
        m_prev1 = m_prev[:, 0:1]
        d = jnp.where(causal, b_col - b_row + li_r, -jnp.inf)
        inter = b_col + m_prev1
        m_t = jnp.maximum(inter, jnp.max(d, axis=-1, keepdims=True))
        s = lax.dot_general(q, k.astype(BF16), _NT, preferred_element_type=F32)
        w = jnp.exp(d - m_t) * s
        e_inter = jnp.exp(inter - m_t)
        c_aug = c_ref[h]
        num = (jnp.dot(w.astype(BF16), vaug, preferred_element_type=F32)
               + e_inter * jnp.dot(q, c_aug.astype(BF16), preferred_element_type=F32))
        den = num[:, M_DV:M_DV + 1]
        hh = num[:, :M_DV] / jnp.maximum(jnp.abs(den), jnp.exp(-m_t))
        mu = jnp.mean(hh, axis=-1, keepdims=True)
        dh = hh - mu
        var = jnp.mean(dh * dh, axis=-1, keepdims=True)
        hn = dh * lax.rsqrt(var + LN_EPS)
        sl = slice(h * M_DV, (h + 1) * M_DV)
        out_ref[:, sl] = hn * ng_ref[:, sl] * _sigmoid(o_ref[:, sl])

        b_last = b_col[L - 1:L, :]
        g_col = b_last - b_col + li_col
        m_new = jnp.maximum(b_last + m_prev1, jnp.max(g_col, axis=0, keepdims=True))
        decay = jnp.exp(b_last + m_prev1 - m_new)
        wk = (jnp.exp(g_col - m_new) * k).astype(BF16)
        c_ref[h] = decay * c_aug + lax.dot_general(wk, vaug, _TN, preferred_element_type=F32)
        m_ref[h] = jnp.broadcast_to(m_new, (1, LANES))


def _mlstm_core(qk, v, o, gates, conv_w, conv_b, norm_g, bn, seq):
    L = CHUNK
    nc = seq // L
    tok = lambda w: pl.BlockSpec((L, w), lambda b, c: (b * nc + c, 0))
    full = lambda r, w: pl.BlockSpec((r, w), lambda b, c: (0, 0))
    return pl.pallas_call(
        _mlstm_kernel,
        grid=(bn, nc),
        in_specs=[tok(N_QK), tok(N_V), tok(D_MODEL), tok(LANES),
                  full(M_CONV, N_QK), full(1, N_QK), full(1, N_V)],
        out_specs=tok(N_V),
        out_shape=jax.ShapeDtypeStruct((bn * seq, N_V), F32),
        scratch_shapes=[
            pltpu.VMEM((L + SUBLANES, N_QK), F32),
            pltpu.VMEM((M_HEADS, M_DQK, M_DV + LANES), F32),
            pltpu.VMEM((M_HEADS, 1, LANES), F32),
        ],
        compiler_params=_cparams(2),
        name="mlstm",
    )(qk, v, o, gates, conv_w, conv_b.reshape(1, N_QK), norm_g.reshape(1, N_V))


def _mlstm_layer(x, mod, w_in, conv_w, conv_b, b_if, norm_g, w_out, lng, lnb, bn, seq):
    sh, sc, gate = mod
    w_qk = w_in[:, :N_QK].astype(BF16)
    w_v = w_in[:, N_QK:N_QK + N_V].astype(BF16)
    w_o = w_in[:, N_QK + N_V:N_QK + N_V + D_MODEL].astype(BF16)
    w_g = jnp.zeros((D_MODEL, LANES), F32).at[:, :2 * M_HEADS].set(w_in[:, N_QK + N_V + D_MODEL:])
    b_g = jnp.zeros((1, LANES), F32).at[0, :2 * M_HEADS].set(b_if)
    z = lambda n: jnp.zeros((1, n), F32)
    qk, v, o, gates = _proj(x, sc, sh, [w_qk, w_v, w_o, w_g], [z(N_QK), z(N_V), z(D_MODEL), b_g],
                            [False, False, False, True], seq)
    hg = _mlstm_core(qk, v, o, gates, conv_w, conv_b, norm_g, bn, seq)
    return _lin_ln(hg, x, gate, w_out.astype(BF16), lng, lnb, seq)


def _s5_in_kernel(x_ref, sc_ref, sh_ref, wt_ref, o_ref):
    h = x_ref[...] * (1.0 + sc_ref[...]) + sh_ref[...]
    ut = lax.dot_general(wt_ref[...], h.astype(BF16), _NT, preferred_element_type=F32)
    for cl in range(o_ref.shape[1]):
        o_ref[:, cl, :, :] = ut[:, cl * LANES:(cl + 1) * LANES].reshape(D_MODEL // SUBLANES, SUBLANES, LANES)


def _s5_in(x, sc, sh, w_t, seq, tm=512):
    t, d = x.shape
    tpb = seq // tm
    nch = tm // LANES
    vec = pl.BlockSpec((None, 1, d), lambda i: (i // tpb, 0, 0))
    return pl.pallas_call(
        _s5_in_kernel,
        grid=(t // tm,),
        in_specs=[pl.BlockSpec((tm, d), lambda i: (i, 0)), vec, vec,
                  pl.BlockSpec((d, d), lambda i: (0, 0))],
        out_specs=pl.BlockSpec((d // SUBLANES, nch, SUBLANES, LANES), lambda i: (0, i, 0, 0)),
        out_shape=jax.ShapeDtypeStruct((d // SUBLANES, t // LANES, SUBLANES, LANES), F32),
        compiler_params=_cparams(1),
        name="s5_in",
    )(x, sc, sh, w_t)


def _cpow(lr, lim, dt, steps):
    mag = jnp.exp(lr * dt * steps)
    ang = lim * dt * steps
    return mag * jnp.cos(ang), mag * jnp.sin(ang)


def _s5_kernel(nc, u_ref, lamr_ref, lamc_ref, ldt_ref, crep_ref, cirep_ref, brt_ref, bit_ref,
               brtt_ref, bitt_ref, crt_ref, cit_ref, y_ref, acc_ref, kt_ref, r_ref):
    L = CHUNK
    G = S5_GROUP
    P = S5_STATE
    nct = acc_ref.shape[1]

    def u_rows(i):
        return u_ref[i // SUBLANES, pl.ds(i % SUBLANES, nct, stride=SUBLANES), :]

    dt = jnp.exp(ldt_ref[...])
    lr_r, lim_r = lamr_ref[0:1, :], lamr_ref[1:2, :]
    lr_c, lim_c = lamc_ref[:, 0:1], lamc_ref[:, 1:2]

    a_re, a_im = _cpow(lr_r, lim_r, dt, 1.0)
    lam_sq = lr_r * lr_r + lim_r * lim_r
    t_re = ((a_re - 1.0) * lr_r + a_im * lim_r) / lam_sq
    t_im = (a_im * lr_r - (a_re - 1.0) * lim_r) / lam_sq

    bbt_re = t_re * brt_ref[...] - t_im * bit_ref[...]
    bbt_im = t_re * bit_ref[...] + t_im * brt_ref[...]
    bbtt_re = t_re * brtt_ref[...] - t_im * bitt_ref[...]
    bbtt_im = t_re * bitt_ref[...] + t_im * brtt_ref[...]
    cb_re = crep_ref[...] * bbtt_re - cirep_ref[...] * bbtt_im
    cb_im = crep_ref[...] * bbtt_im + cirep_ref[...] * bbtt_re

    lag = lax.broadcasted_iota(I32, (P, L), 1).astype(F32)
    p0_re, p0_im = _cpow(lr_c, lim_c, dt, lag)
    p1_re, p1_im = _cpow(lr_c, lim_c, dt, lag + 1.0)
    kt_ref[...] = _dot3(cb_re, p0_re) - _dot3(cb_im, p0_im)

    for o in range(G):
        cr_o = crt_ref[:, o:o + 1]
        ci_o = cit_ref[:, o:o + 1]
        r_ref[0:P, o * L:(o + 1) * L] = (cr_o * p1_re - ci_o * p1_im).astype(BF16)
        r_ref[P:2 * P, o * L:(o + 1) * L] = (-(cr_o * p1_im + ci_o * p1_re)).astype(BF16)

    back = (L - 1.0) - lax.broadcasted_iota(I32, (L, P), 0).astype(F32)
    ps_re, ps_im = _cpow(lr_r, lim_r, dt, back)
    e_acc = jnp.zeros((nct, 2 * P), F32)
    for i in range(G):
        a_i = u_rows(i).astype(BF16)
        bi_re = bbt_re[i:i + 1, :]
        bi_im = bbt_im[i:i + 1, :]
        w_i = jnp.concatenate([ps_re * bi_re - ps_im * bi_im, ps_re * bi_im + ps_im * bi_re], axis=1)
        e_acc = e_acc + jnp.dot(a_i, w_i.astype(BF16), preferred_element_type=F32)

    chunk_in_seq = lax.broadcasted_iota(I32, (nct, 2 * P), 0) % nc
    x = e_acc
    dist = 1
    while dist < nc:
        d_re, d_im = _cpow(lr_r, lim_r, dt, float(L * dist))
        m1 = jnp.concatenate([d_re, d_re], axis=1)
        m2 = jnp.concatenate([-d_im, d_im], axis=1)
        shifted = jnp.where(chunk_in_seq >= dist, pltpu.roll(x, dist, 0), 0.0)
        x = x + shifted * m1 + pltpu.roll(shifted, P, 1) * m2
        dist *= 2
    x_in = jnp.where(chunk_in_seq >= 1, pltpu.roll(x, 1, 0), 0.0)

    x_in = x_in.astype(BF16)
    for op in range(G // 2):
        acc_ref[op] = jnp.dot(x_in, r_ref[:, op * 2 * L:(op + 1) * 2 * L], preferred_element_type=F32)

    srow = lax.broadcasted_iota(I32, (L, L), 0)
    tcol = lax.broadcasted_iota(I32, (L, L), 1)
    keep = tcol >= srow

    def toeplitz(o, i):
        k_row = jnp.broadcast_to(kt_ref[pl.ds(o * G + i, 1), :], (L, L))
        return jnp.where(keep, pltpu.roll(k_row, 0, 1, stride=1, stride_axis=0), 0.0).astype(BF16)

    for ip in range(G // 2):
        i0, i1 = 2 * ip, 2 * ip + 1
        a_pair = jnp.concatenate([u_rows(i0), u_rows(i1)], axis=1).astype(BF16)

        def out_pair(op, _):
            o0, o1 = 2 * op, 2 * op + 1
            toe = jnp.concatenate(
                [jnp.concatenate([toeplitz(o0, i0), toeplitz(o1, i0)], axis=1),
                 jnp.concatenate([toeplitz(o0, i1), toeplitz(o1, i1)], axis=1)], axis=0)
            acc_ref[op] += jnp.dot(a_pair, toe, preferred_element_type=F32)
            return 0

        lax.fori_loop(0, G // 2, out_pair, 0)

    for o in range(G):
        y_ref[o // SUBLANES, pl.ds(o % SUBLANES, nct, stride=SUBLANES), :] = (
            acc_ref[o // 2, :, (o % 2) * L:(o % 2 + 1) * L])


def _s5_core(u4, lam_re, lam_im, log_dt, b_re, b_im, c_re, c_im, bn, seq):
    G, P, L = S5_GROUP, S5_STATE, CHUNK
    ng = S5_GROUPS
    nc = seq // L
    nct = bn * nc
    rt = G // SUBLANES
    u3 = u4.reshape(D_MODEL // SUBLANES, nct * SUBLANES, L)
    lam_r = jnp.stack([lam_re, lam_im], axis=1)
    lam_c = jnp.stack([lam_re, lam_im], axis=2)
    brt = jnp.swapaxes(b_re, 1, 2)
    bit = jnp.swapaxes(b_im, 1, 2)
    spec = lambda a: pl.BlockSpec((None,) + a.shape[1:], lambda g: (g,) + (0,) * (a.ndim - 1))
    params = [
        lam_r, lam_c, log_dt.reshape(ng, 1, 1),
        jnp.repeat(c_re, G, axis=1), jnp.repeat(c_im, G, axis=1),
        brt, bit,
        jnp.tile(brt, (1, G, 1)), jnp.tile(bit, (1, G, 1)),
        jnp.swapaxes(c_re, 1, 2), jnp.swapaxes(c_im, 1, 2),
    ]
    y3 = pl.pallas_call(
        functools.partial(_s5_kernel, nc),
        grid=(ng,),
        in_specs=[pl.BlockSpec((rt, nct * SUBLANES, L), lambda g: (g, 0, 0))] + [spec(a) for a in params],
        out_specs=pl.BlockSpec((rt, nct * SUBLANES, L), lambda g: (g, 0, 0)),
        out_shape=jax.ShapeDtypeStruct(u3.shape, F32),
        scratch_shapes=[
            pltpu.VMEM((G // 2, nct, 2 * L), F32),
            pltpu.VMEM((G * G, L), F32),
            pltpu.VMEM((2 * P, G * L), BF16),
        ],
        compiler_params=_cparams(1),
        name="s5_core",
    )(u3, *params)
    return u3, y3


def _s5_out_kernel(u_ref, y_ref, dsk_ref, x_ref, g_ref, w_ref, lng_ref, lnb_ref, o_ref):
    nch = u_ref.shape[1] // SUBLANES
    for cl in range(nch):
        rows = slice(cl * SUBLANES, (cl + 1) * SUBLANES)
        v = _gelu_tanh(y_ref[:, rows, :] + dsk_ref[...] * u_ref[:, rows, :])
        vt = v.reshape(D_MODEL, LANES).T
        z = jnp.dot(vt.astype(BF16), w_ref[...], preferred_element_type=F32)
        y = z[:, :D_MODEL] * _sigmoid(z[:, D_MODEL:])
        tok = slice(cl * LANES, (cl + 1) * LANES)
        o_ref[tok, :] = _layer_norm(DN_ALPHA * x_ref[tok, :] + g_ref[...] * y, lng_ref[...], lnb_ref[...])


def _s5_out(u3, y3, d_skip, x, gate, w_glu, lng, lnb, seq, tm=512):
    t, d = x.shape
    tpb = seq // tm
    nch = tm // LANES
    dsk = jnp.broadcast_to(d_skip.reshape(d // SUBLANES, SUBLANES, 1), (d // SUBLANES, SUBLANES, LANES))
    row = pl.BlockSpec((1, d), lambda i: (0, 0))
    tile3 = pl.BlockSpec((d // SUBLANES, nch * SUBLANES, LANES), lambda i: (0, i, 0))
    return pl.pallas_call(
        _s5_out_kernel,
        grid=(t // tm,),
        in_specs=[tile3, tile3,
                  pl.BlockSpec(dsk.shape, lambda i: (0, 0, 0)),
                  pl.BlockSpec((tm, d), lambda i: (i, 0)),
                  pl.BlockSpec((None, 1, d), lambda i: (i // tpb, 0, 0)),
                  pl.BlockSpec(w_glu.shape, lambda i: (0, 0)),
                  row, row],
        out_specs=pl.BlockSpec((tm, d), lambda i: (i, 0)),
        out_shape=jax.ShapeDtypeStruct((t, d), F32),
        compiler_params=_cparams(1),
        name="s5_out",
    )(u3, y3, dsk, x, gate, w_glu, lng.reshape(1, d), lnb.reshape(1, d))


def _s5_layer(x, mod, w_in, lam_re, lam_im, log_dt, b_re, b_im, c_re, c_im, d_skip, w_glu, lng, lnb, bn, seq):
    sh, sc, gate = mod
    u4 = _s5_in(x, sc, sh, w_in.T.astype(BF16), seq)
    u3, y3 = _s5_core(u4, lam_re, lam_im, log_dt, b_re, b_im, c_re, c_im, bn, seq)
    return _s5_out(u3, y3, d_skip, x, gate, w_glu.astype(BF16), lng, lnb, seq)


def _swa_kernel(sink_ref, q_ref, kvc_ref, kvp_ref, o_ref):
    L = WINDOW
    n = pl.program_id(1)
    qi = lax.broadcasted_iota(I32, (L, 2 * L), 0)
    kj = lax.broadcasted_iota(I32, (L, 2 * L), 1)
    kmin = jnp.where(n > 0, 0, L)
    valid = (kj > qi) & (kj <= qi + L) & (kj >= kmin)
    nkv = A_KV_HEADS * A_HEAD_DIM
    for h in range(A_KV_HEADS):
        ks = slice(h * A_HEAD_DIM, (h + 1) * A_HEAD_DIM)
        vs = slice(nkv + h * A_HEAD_DIM, nkv + (h + 1) * A_HEAD_DIM)
        kb = jnp.concatenate([kvp_ref[:, ks], kvc_ref[:, ks]], axis=0).astype(BF16)
        vb = jnp.concatenate([kvp_ref[:, vs], kvc_ref[:, vs]], axis=0).astype(BF16)
        for g in range(A_GROUP):
            hd = h * A_GROUP + g
            cs = slice(hd * A_HEAD_DIM, (hd + 1) * A_HEAD_DIM)
            q = (q_ref[:, cs] * (A_HEAD_DIM ** -0.5)).astype(BF16)
            s = lax.dot_general(q, kb, _NT, preferred_element_type=F32)
            s = jnp.where(valid, s, -jnp.inf)
            sink = sink_ref[hd]
            m = jnp.maximum(jnp.max(s, axis=-1, keepdims=True), sink)
            p = jnp.exp(s - m)
            den = jnp.sum(p, axis=-1, keepdims=True) + jnp.exp(sink - m)
            o_ref[:, cs] = jnp.dot(p.astype(BF16), vb, preferred_element_type=F32) / den


def _swa_core(qkv, sinks, bn, seq):
    L = WINDOW
    nb = seq // L
    nq = A_HEADS * A_HEAD_DIM
    nkv2 = 2 * A_KV_HEADS * A_HEAD_DIM
    kvcol = nq // nkv2
    return pl.pallas_call(
        _swa_kernel,
        grid_spec=pltpu.PrefetchScalarGridSpec(
            num_scalar_prefetch=1,
            grid=(bn, nb),
            in_specs=[
                pl.BlockSpec((L, nq), lambda b, n, s: (b * nb + n, 0)),
                pl.BlockSpec((L, nkv2), lambda b, n, s: (b * nb + n, kvcol)),
                pl.BlockSpec((L, nkv2), lambda b, n, s: (b * nb + jnp.maximum(n - 1, 0), kvcol)),
            ],
            out_specs=pl.BlockSpec((L, nq), lambda b, n, s: (b * nb + n, 0)),
        ),
        out_shape=jax.ShapeDtypeStruct((bn * seq, nq), F32),
        compiler_params=_cparams(2),
        name="swa",
    )(sinks, qkv, qkv, qkv)


def _swa_layer(x, mod, w_qkv, b_qkv, sinks, w_o, lng, lnb, bn, seq):
    sh, sc, gate = mod
    (qkv,) = _proj(x, sc, sh, [w_qkv.astype(BF16)], [b_qkv.reshape(1, -1)], [False], seq)
    o = _swa_core(qkv, sinks, bn, seq)
    return _lin_ln(o, x, gate, w_o.astype(BF16), lng, lnb, seq)


ROUTER_OFF = SUBLANES


def _router_kernel(x_ref, sc_ref, sh_ref, w_ref, b_ref, e_ref, rank_ref, wcol_ref, cnt_ref, carry_ref):
    tm = x_ref.shape[0]
    i = pl.program_id(0)

    @pl.when(i == 0)
    def _():
        carry_ref[...] = jnp.zeros_like(carry_ref)

    h = x_ref[...] * (1.0 + sc_ref[...]) + sh_ref[...]
    logits = _dot3(h, w_ref[...]) + b_ref[...]
    lt = logits.T
    gl = [lt[k:k + 1, :] for k in range(E_GROUPS)]
    gmax = jnp.maximum(jnp.maximum(gl[0], gl[1]), jnp.maximum(gl[2], gl[3]))
    g_sel = jnp.where(gl[0] == gmax, 0, jnp.where(gl[1] == gmax, 1, jnp.where(gl[2] == gmax, 2, 3)))
    p_g = 1.0 / (jnp.exp(gl[0] - gmax) + jnp.exp(gl[1] - gmax) + jnp.exp(gl[2] - gmax) + jnp.exp(gl[3] - gmax))
    el = [lt[ROUTER_OFF + k * E_PER_GROUP:ROUTER_OFF + (k + 1) * E_PER_GROUP, :] for k in range(E_GROUPS)]
    eg = jnp.where(g_sel == 0, el[0], jnp.where(g_sel == 1, el[1], jnp.where(g_sel == 2, el[2], el[3])))
    sub = lax.broadcasted_iota(I32, (E_PER_GROUP, tm), 0).astype(F32)
    v1 = jnp.max(eg, axis=0, keepdims=True)
    i1 = jnp.min(jnp.where(eg == v1, sub, float(E_PER_GROUP)), axis=0, keepdims=True)
    eg2 = jnp.where(sub == i1, -jnp.inf, eg)
    v2 = jnp.max(eg2, axis=0, keepdims=True)
    i2 = jnp.min(jnp.where(eg2 == v2, sub, float(E_PER_GROUP)), axis=0, keepdims=True)
    t2 = jnp.exp(v2 - v1)
    w1 = p_g / (1.0 + t2)
    w2 = w1 * t2
    e1 = g_sel * E_PER_GROUP + i1.astype(I32)
    e2 = g_sel * E_PER_GROUP + i2.astype(I32)

    sub8 = lax.broadcasted_iota(I32, (SUBLANES, tm), 0)
    e_ref[...] = jnp.where(sub8 == 0, e1, jnp.where(sub8 == 1, e2, 0))
    subl = lax.broadcasted_iota(I32, (LANES, tm), 0)
    wcol_ref[...] = jnp.where(subl == 0, w1, jnp.where(subl == 1, w2, 0.0)).T

    eid = lax.broadcasted_iota(I32, (N_EXPERTS, tm), 0)
    oh1 = eid == e1
    oh2 = eid == e2
    ohs = jnp.where(oh1 | oh2, 1.0, 0.0)
    upper = (lax.broadcasted_iota(I32, (tm, tm), 0) <= lax.broadcasted_iota(I32, (tm, tm), 1)).astype(BF16)
    incl = jnp.dot(ohs.astype(BF16), upper, preferred_element_type=F32)
    base = incl - ohs + carry_ref[:, 0:1]
    r1 = jnp.sum(jnp.where(oh1, base, 0.0), axis=0, keepdims=True)
    r2 = jnp.sum(jnp.where(oh2, base, 0.0), axis=0, keepdims=True)
    rank_ref[...] = jnp.where(sub8 == 0, r1, jnp.where(sub8 == 1, r2, 0.0)).astype(I32)
    carry_ref[...] = carry_ref[...] + jnp.sum(ohs, axis=1, keepdims=True)
    cnt_ref[...] = carry_ref[...]


def _router(x, sc, sh, w_group, b_group, w_router, b_router, seq, tm=512):
    t, d = x.shape
    tpb = seq // tm
    w = jnp.zeros((d, LANES), F32).at[:, :E_GROUPS].set(w_group).at[:, ROUTER_OFF:ROUTER_OFF + N_EXPERTS].set(w_router)
    b = jnp.zeros((1, LANES), F32).at[0, :E_GROUPS].set(b_group).at[0, ROUTER_OFF:ROUTER_OFF + N_EXPERTS].set(b_router)
    vec = pl.BlockSpec((None, 1, d), lambda i: (i // tpb, 0, 0))
    rows = pl.BlockSpec((SUBLANES, tm), lambda i: (0, i))
    return pl.pallas_call(
        _router_kernel,
        grid=(t // tm,),
        in_specs=[pl.BlockSpec((tm, d), lambda i: (i, 0)), vec, vec,
                  pl.BlockSpec((d, LANES), lambda i: (0, 0)), pl.BlockSpec((1, LANES), lambda i: (0, 0))],
        out_specs=[rows, rows, pl.BlockSpec((tm, LANES), lambda i: (i, 0)),
                   pl.BlockSpec((N_EXPERTS, LANES), lambda i: (0, 0))],
        out_shape=[jax.ShapeDtypeStruct((SUBLANES, t), I32), jax.ShapeDtypeStruct((SUBLANES, t), I32),
                   jax.ShapeDtypeStruct((t, LANES), F32), jax.ShapeDtypeStruct((N_EXPERTS, LANES), F32)],
        scratch_shapes=[pltpu.VMEM((N_EXPERTS, LANES), F32)],
        compiler_params=_cparams(1),
        name="router",
    )(x, sc, sh, w, b)


def _pos_kernel(e_ref, rank_ref, pstart_ref, pos_ref):
    tm = e_ref.shape[1]
    eid = lax.broadcasted_iota(I32, (N_EXPERTS, tm), 0)
    start = pstart_ref[:, 0:1]
    rows = []
    for k in range(TOP_K):
        base = jnp.sum(jnp.where(eid == e_ref[k:k + 1, :], start, 0.0), axis=0, keepdims=True)
        rows.append(base.astype(I32) + rank_ref[k:k + 1, :])
    sub8 = lax.broadcasted_iota(I32, (SUBLANES, tm), 0)
    pos_ref[...] = jnp.where(sub8 == 0, rows[0], jnp.where(sub8 == 1, rows[1], 0))


def _positions(e, rank, pstart, tm=2048):
    t = e.shape[1]
    tm = min(tm, t)
    rows = pl.BlockSpec((SUBLANES, tm), lambda i: (0, i))
    return pl.pallas_call(
        _pos_kernel,
        grid=(t // tm,),
        in_specs=[rows, rows, pl.BlockSpec((N_EXPERTS, LANES), lambda i: (0, 0))],
        out_specs=rows,
        out_shape=jax.ShapeDtypeStruct((SUBLANES, t), I32),
        compiler_params=_cparams(1),
        name="moe_pos",
    )(e, rank, jnp.broadcast_to(pstart.astype(F32)[:, None], (N_EXPERTS, LANES)))


SLAB = D_MODEL // LANES


def _to_slabs(ref, val):
    rows = val.shape[0]
    for cidx in range(SLAB):
        ref[pl.ds(cidx, rows, stride=SLAB), :] = val[:, cidx * LANES:(cidx + 1) * LANES]


def _from_slabs(ref, rows):
    return jnp.concatenate([ref[pl.ds(cidx, rows, stride=SLAB), :] for cidx in range(SLAB)], axis=1)


def _row_copy(src, src_row, dst, dst_row, sem):
    s0 = pl.multiple_of(src_row * SLAB, SLAB)
    d0 = pl.multiple_of(dst_row * SLAB, SLAB)
    return pltpu.make_async_copy(src.at[pl.ds(s0, SLAB), :], dst.at[pl.ds(d0, SLAB), :], sem)


def _dispatch_kernel(pend_ref, x_ref, sc_ref, sh_ref, pos_ref, buf_ref, hbuf, zbuf, sem, zsem):
    tm = x_ref.shape[0]
    i = pl.program_id(0)

    def pad_copy(e):
        start = pl.multiple_of((pend_ref[e] - MOE_BM) * SLAB, MOE_BM * SLAB)
        return pltpu.make_async_copy(zbuf, buf_ref.at[pl.ds(start, MOE_BM * SLAB), :], zsem)

    def padded(e):
        return pend_ref[e] - (pend_ref[e - 1] if e > 0 else 0)

    @pl.when(i == 0)
    def _():
        zbuf[...] = jnp.zeros_like(zbuf)
        for e in range(N_EXPERTS):
            @pl.when(padded(e) > 0)
            def _():
                pad_copy(e).start()
        for e in range(N_EXPERTS):
            @pl.when(padded(e) > 0)
            def _():
                pad_copy(e).wait()

        def tail_copy(j):
            start = pl.multiple_of(j * (MOE_BM * SLAB), MOE_BM * SLAB)
            return pltpu.make_async_copy(zbuf, buf_ref.at[pl.ds(start, MOE_BM * SLAB), :], zsem)

        first_unused = pend_ref[N_EXPERTS - 1] // MOE_BM
        n_blocks = buf_ref.shape[0] // (MOE_BM * SLAB)
        lax.fori_loop(first_unused, n_blocks, lambda j, c: (tail_copy(j).start(), c)[1], 0)
        lax.fori_loop(first_unused, n_blocks, lambda j, c: (tail_copy(j).wait(), c)[1], 0)

    _to_slabs(hbuf, x_ref[...] * (1.0 + sc_ref[...]) + sh_ref[...])

    def issue(r, _):
        for k in range(TOP_K):
            _row_copy(hbuf, r, buf_ref, pos_ref[k, r], sem).start()
        return 0

    lax.fori_loop(0, tm, issue, 0)

    def drain(r, _):
        for k in range(TOP_K):
            _row_copy(hbuf, 0, buf_ref, 0, sem).wait()
        return 0

    lax.fori_loop(0, tm, drain, 0)


def _dispatch(x, sc, sh, pos, pend, n_rows, seq, tm=256):
    t, d = x.shape
    tpb = seq // tm
    nt = t // tm
    pos3 = pos[:TOP_K].reshape(TOP_K, nt, tm).transpose(1, 0, 2)
    vec = pl.BlockSpec((None, 1, d), lambda i, p: (i // tpb, 0, 0))
    return pl.pallas_call(
        _dispatch_kernel,
        grid_spec=pltpu.PrefetchScalarGridSpec(
            num_scalar_prefetch=1,
            grid=(nt,),
            in_specs=[pl.BlockSpec((tm, d), lambda i, p: (i, 0)), vec, vec,
                      pl.BlockSpec((None, TOP_K, tm), lambda i, p: (i, 0, 0), memory_space=pltpu.SMEM)],
            out_specs=pl.BlockSpec(memory_space=pl.ANY),
            scratch_shapes=[pltpu.VMEM((tm * SLAB, LANES), F32), pltpu.VMEM((MOE_BM * SLAB, LANES), F32),
                            pltpu.SemaphoreType.DMA(()), pltpu.SemaphoreType.DMA(())],
        ),
        out_shape=jax.ShapeDtypeStruct((n_rows * SLAB, LANES), F32),
        compiler_params=_cparams(1),
        name="moe_dispatch",
    )(pend, x, sc, sh, pos3)


def _experts_kernel(blk_e_ref, nb_ref, x_ref, w1_ref, w3_ref, w2_ref, y_ref):
    @pl.when(pl.program_id(0) < nb_ref[0])
    def _():
        xb = _from_slabs(x_ref, MOE_BM).astype(BF16)
        h1 = jnp.dot(xb, w1_ref[...], preferred_element_type=F32)
        h3 = jnp.dot(xb, w3_ref[...], preferred_element_type=F32)
        a = (_silu(h1) * h3).astype(BF16)
        _to_slabs(y_ref, jnp.dot(a, w2_ref[...], preferred_element_type=F32))

    @pl.when(pl.program_id(0) >= nb_ref[0])
    def _():
        y_ref[...] = jnp.zeros_like(y_ref)


def _experts(buf, blk_e, nb_used, w1, w3, w2):
    nblk = buf.shape[0] // (MOE_BM * SLAB)
    d, de = w1.shape[1], w1.shape[2]
    row_blk = lambda j, be, nb: (jnp.minimum(j, nb[0] - 1), 0)
    wsel = lambda j, be, nb: (be[j], 0, 0)
    return pl.pallas_call(
        _experts_kernel,
        grid_spec=pltpu.PrefetchScalarGridSpec(
            num_scalar_prefetch=2,
            grid=(nblk,),
            in_specs=[pl.BlockSpec((MOE_BM * SLAB, LANES), row_blk),
                      pl.BlockSpec((None, d, de), wsel), pl.BlockSpec((None, d, de), wsel),
                      pl.BlockSpec((None, de, d), wsel)],
            out_specs=pl.BlockSpec((MOE_BM * SLAB, LANES), lambda j, be, nb: (j, 0)),
        ),
        out_shape=jax.ShapeDtypeStruct(buf.shape, F32),
        compiler_params=_cparams(1),
        name="moe_experts",
    )(blk_e, nb_used, buf, w1, w3, w2)


def _combine_kernel(pos_ref, y_ref, wcol_ref, x_ref, g_ref, lng_ref, lnb_ref, o_ref, ybuf, sem):
    tm = x_ref.shape[0]

    def issue(r, _):
        for k in range(TOP_K):
            _row_copy(y_ref, pos_ref[k, r], ybuf.at[k], r, sem).start()
        return 0

    lax.fori_loop(0, tm, issue, 0)

    def drain(r, _):
        for k in range(TOP_K):
            _row_copy(y_ref, 0, ybuf.at[k], 0, sem).wait()
        return 0

    lax.fori_loop(0, tm, drain, 0)
    y = wcol_ref[:, 0:1] * _from_slabs(ybuf.at[0], tm) + wcol_ref[:, 1:2] * _from_slabs(ybuf.at[1], tm)
    o_ref[...] = _layer_norm(DN_ALPHA * x_ref[...] + g_ref[...] * y, lng_ref[...], lnb_ref[...])


def _combine(y, pos, wcol, x, gate, lng, lnb, seq, tm=256):
    t, d = x.shape
    tpb = seq // tm
    nt = t // tm
    pos3 = pos[:TOP_K].reshape(TOP_K, nt, tm).transpose(1, 0, 2)
    row = pl.BlockSpec((1, d), lambda i: (0, 0))
    return pl.pallas_call(
        _combine_kernel,
        grid=(nt,),
        in_specs=[pl.BlockSpec((None, TOP_K, tm), lambda i: (i, 0, 0), memory_space=pltpu.SMEM),
                  pl.BlockSpec(memory_space=pl.ANY),
                  pl.BlockSpec((tm, LANES), lambda i: (i, 0)),
                  pl.BlockSpec((tm, d), lambda i: (i, 0)),
                  pl.BlockSpec((None, 1, d), lambda i: (i // tpb, 0, 0)),
                  row, row],
        out_specs=pl.BlockSpec((tm, d), lambda i: (i, 0)),
        out_shape=jax.ShapeDtypeStruct((t, d), F32),
        scratch_shapes=[pltpu.VMEM((TOP_K, tm * SLAB, LANES), F32), pltpu.SemaphoreType.DMA(())],
        compiler_params=_cparams(1),
        name="moe_combine",
    )(pos3, y, wcol, x, gate, lng.reshape(1, d), lnb.reshape(1, d))


def _moe_layer(x, mod, w_group, b_group, w_router, b_router, w1, w3, w2, lng, lnb, seq):
    sh, sc, gate = mod
    t, d = x.shape
    e, rank, wcol, cnt = _router(x, sc, sh, w_group, b_group, w_router, b_router, seq)
    counts = cnt[:, 0].astype(I32)
    padded = (counts + MOE_BM - 1) // MOE_BM * MOE_BM
    pend = jnp.cumsum(padded)
    pstart = pend - padded
    n_rows = t * TOP_K + N_EXPERTS * MOE_BM
    nblk = n_rows // MOE_BM
    blk_e = jnp.clip(jnp.searchsorted(pend, jnp.arange(nblk, dtype=I32) * MOE_BM, side='right'),
                     0, N_EXPERTS - 1).astype(I32)
    nb_used = (pend[-1:] // MOE_BM).astype(I32)
    pos = _positions(e, rank, pstart)
    buf = _dispatch(x, sc, sh, pos, pend.astype(I32), n_rows, seq)
    y = _experts(buf, blk_e, nb_used, w1.astype(BF16), w3.astype(BF16), w2.astype(BF16))
    return _combine(y, pos, wcol, x, gate, lng, lnb, seq)


def kernel(x, c, ada_w, ada_b, ln1_g, ln1_b, ln2_g, ln2_b, mlstm_w_in, mlstm_conv_w, mlstm_conv_b, mlstm_b_if, mlstm_norm_g, mlstm_w_out, s5_w_in, s5_lam_re, s5_lam_im, s5_log_dt, s5_b_re, s5_b_im, s5_c_re, s5_c_im, s5_d, s5_w_glu, swa_w_qkv, swa_b_qkv, swa_sinks, swa_w_o, moe_w_group, moe_b_group, moe_w_router, moe_b_router, moe_w1, moe_w3, moe_w2):
    bn, seq, d = x.shape
    depth = ada_w.shape[0]
    mods = _ada(c, ada_w, ada_b).reshape(depth, bn, 6, 1, d)
    xt = x.reshape(bn * seq, d)
    for i in range(depth):
        m1 = tuple(mods[i, :, k] for k in range(3))
        m2 = tuple(mods[i, :, k] for k in range(3, 6))
        kind, j = i % N_MIXERS, i // N_MIXERS
        if kind == 0:
            xt = _mlstm_layer(xt, m1, mlstm_w_in[j], mlstm_conv_w[j], mlstm_conv_b[j], mlstm_b_if[j],
                              mlstm_norm_g[j], mlstm_w_out[j], ln1_g[i], ln1_b[i], bn, seq)
        elif kind == 1:
            xt = _s5_layer(xt, m1, s5_w_in[j], s5_lam_re[j], s5_lam_im[j], s5_log_dt[j], s5_b_re[j], s5_b_im[j],
                           s5_c_re[j], s5_c_im[j], s5_d[j], s5_w_glu[j], ln1_g[i], ln1_b[i], bn, seq)
        else:
            xt = _swa_layer(xt, m1, swa_w_qkv[j], swa_b_qkv[j], swa_sinks[j], swa_w_o[j],
                            ln1_g[i], ln1_b[i], bn, seq)
        xt = _moe_layer(xt, m2, moe_w_group[i], moe_b_group[i], moe_w_router[i], moe_b_router[i],
                        moe_w1[i], moe_w3[i], moe_w2[i], ln2_g[i], ln2_b[i], seq)
    return xt.reshape(bn, seq, d)
```

```python
import functools
import math

import jax
import jax.numpy as jnp
from jax import lax
from jax.experimental import pallas as pl
from jax.experimental.pallas import tpu as pltpu

F32 = jnp.float32
BF16 = jnp.bfloat16
I32 = jnp.int32

D_MODEL = 1024
DEPTH = 4
N_MIXERS = 3
DN_ALPHA = (2 * DEPTH) ** 0.25
LN_EPS = 1e-5

M_HEADS = 4
M_DQK = D_MODEL // 8
M_DV = D_MODEL // M_HEADS
M_CONV = 4
N_QK = 2 * M_HEADS * M_DQK
N_V = M_HEADS * M_DV

S5_GROUP = 16
S5_GROUPS = D_MODEL // S5_GROUP
S5_STATE = 64

A_HEADS = 16
A_KV_HEADS = 4
A_GROUP = A_HEADS // A_KV_HEADS
A_HEAD_DIM = D_MODEL // A_HEADS
WINDOW = 128

E_GROUPS = 4
E_PER_GROUP = 8
N_EXPERTS = E_GROUPS * E_PER_GROUP
TOP_K = 2
D_EXPERT = 512

LANES = 128
SUBLANES = 8
CHUNK = LANES
MOE_BM = 256
VMEM_LIMIT = 56 * 1024 * 1024


def _cparams(n_axes):
    return pltpu.CompilerParams(dimension_semantics=("arbitrary",) * n_axes, vmem_limit_bytes=VMEM_LIMIT)


def _split_bf16(a):
    hi = a.astype(BF16)
    lo = (a - hi.astype(F32)).astype(BF16)
    return hi, lo


def _dot3(a, b, dims=None):
    ah, al = _split_bf16(a)
    bh, bl = _split_bf16(b)
    if dims is None:
        dot = lambda x, y: jnp.dot(x, y, preferred_element_type=F32)
    else:
        dot = lambda x, y: lax.dot_general(x, y, dims, preferred_element_type=F32)
    return dot(ah, bh) + (dot(ah, bl) + dot(al, bh))


def _bdot(a, b):
    return jnp.dot(a.astype(BF16), b.astype(BF16), preferred_element_type=F32)


_NT = (((1,), (1,)), ((), ()))
_TN = (((0,), (0,)), ((), ()))


def _layer_norm(v, g, b):
    mu = jnp.mean(v, axis=-1, keepdims=True)
    d = v - mu
    var = jnp.mean(d * d, axis=-1, keepdims=True)
    return d * lax.rsqrt(var + LN_EPS) * g + b


def _sigmoid(x):
    return 1.0 / (1.0 + jnp.exp(-x))


def _silu(x):
    return x * _sigmoid(x)


def _log_sigmoid(x):
    return jnp.minimum(x, 0.0) - jnp.log1p(jnp.exp(-jnp.abs(x)))


def _gelu_tanh(x):
    return 0.5 * x * (1.0 + jnp.tanh(math.sqrt(2.0 / math.pi) * (x + 0.044715 * (x * x * x))))


def _ada_kernel(c_ref, w_ref, b_ref, o_ref):
    cond = _silu(c_ref[...])
    o_ref[...] = _dot3(cond, w_ref[...]) + b_ref[...]


def _ada(c, ada_w, ada_b):
    bn, d = c.shape
    depth, _, n = ada_w.shape
    tn = 1536
    c8 = jnp.zeros((SUBLANES, d), F32).at[:bn].set(c)
    out = pl.pallas_call(
        _ada_kernel,
        grid=(depth, n // tn),
        in_specs=[
            pl.BlockSpec((SUBLANES, d), lambda l, j: (0, 0)),
            pl.BlockSpec((None, d, tn), lambda l, j: (l, 0, j)),
            pl.BlockSpec((None, 1, tn), lambda l, j: (l, 0, j)),
        ],
        out_specs=pl.BlockSpec((None, SUBLANES, tn), lambda l, j: (l, 0, j)),
        out_shape=jax.ShapeDtypeStruct((depth, SUBLANES, n), F32),
        compiler_params=_cparams(2),
        name="ada",
    )(c8, ada_w, ada_b.reshape(depth, 1, n))
    return out[:, :bn]


def _proj_kernel(precise, x_ref, sc_ref, sh_ref, *refs):
    n = len(precise)
    w_refs, b_refs, o_refs = refs[:n], refs[n:2 * n], refs[2 * n:]
    h = x_ref[...] * (1.0 + sc_ref[...]) + sh_ref[...]
    hb = h.astype(BF16)
    for w_ref, b_ref, o_ref, p in zip(w_refs, b_refs, o_refs, precise):
        if p:
            r = _dot3(h, w_ref[...])
        else:
            r = jnp.dot(hb, w_ref[...], preferred_element_type=F32)
        o_ref[...] = r + b_ref[...]


def _proj(x, sc, sh, ws, bs, precise, seq, tm=512):
    t, d = x.shape
    tpb = seq // tm
    vec = pl.BlockSpec((None, 1, d), lambda i: (i // tpb, 0, 0))
    in_specs = [pl.BlockSpec((tm, d), lambda i: (i, 0)), vec, vec]
    in_specs += [pl.BlockSpec(w.shape, lambda i: (0, 0)) for w in ws]
    in_specs += [pl.BlockSpec(b.shape, lambda i: (0, 0)) for b in bs]
    return pl.pallas_call(
        functools.partial(_proj_kernel, tuple(precise)),
        grid=(t // tm,),
        in_specs=in_specs,
        out_specs=[pl.BlockSpec((tm, w.shape[1]), lambda i: (i, 0)) for w in ws],
        out_shape=[jax.ShapeDtypeStruct((t, w.shape[1]), F32) for w in ws],
        compiler_params=_cparams(1),
        name="proj",
    )(x, sc, sh, *ws, *bs)


def _lin_ln_kernel(a_ref, x_ref, g_ref, w_ref, lng_ref, lnb_ref, o_ref):
    y = jnp.dot(a_ref[...].astype(BF16), w_ref[...], preferred_element_type=F32)
    o_ref[...] = _layer_norm(DN_ALPHA * x_ref[...] + g_ref[...] * y, lng_ref[...], lnb_ref[...])


def _lin_ln(a, x, gate, w, lng, lnb, seq, tm=512):
    t, d = x.shape
    k = a.shape[1]
    tpb = seq // tm
    row = pl.BlockSpec((1, d), lambda i: (0, 0))
    return pl.pallas_call(
        _lin_ln_kernel,
        grid=(t // tm,),
        in_specs=[
            pl.BlockSpec((tm, k), lambda i: (i, 0)),
            pl.BlockSpec((tm, d), lambda i: (i, 0)),
            pl.BlockSpec((None, 1, d), lambda i: (i // tpb, 0, 0)),
            pl.BlockSpec((k, d), lambda i: (0, 0)),
            row, row,
        ],
        out_specs=pl.BlockSpec((tm, d), lambda i: (i, 0)),
        out_shape=jax.ShapeDtypeStruct((t, d), F32),
        compiler_params=_cparams(1),
        name="lin_ln",
    )(a, x, gate, w, lng.reshape(1, d), lnb.reshape(1, d))


def _mlstm_kernel(qk_ref, v_ref, o_ref, gates_ref, cw_ref, cb_ref, ng_ref, out_ref,
                  qkbuf, c_ref, m_ref):
    L = CHUNK
    blk = pl.program_id(1)

    @pl.when(blk == 0)
    def _():
        qkbuf[0:SUBLANES, :] = jnp.zeros((SUBLANES, N_QK), F32)
        c_ref[...] = jnp.zeros_like(c_ref)
        m_ref[...] = jnp.zeros_like(m_ref)

    qkbuf[SUBLANES:SUBLANES + L, :] = qk_ref[...]
    conv = cb_ref[...] + cw_ref[0:1, :] * qkbuf[pl.ds(SUBLANES - 3, L), :]
    for j in range(1, M_CONV):
        conv = conv + cw_ref[j:j + 1, :] * qkbuf[pl.ds(SUBLANES - 3 + j, L), :]
    qk = _silu(conv)
    qkbuf[0:SUBLANES, :] = qkbuf[L:L + SUBLANES, :]

    gates = gates_ref[...]
    gates_t = gates.T
    lf_col = _log_sigmoid(gates)
    lf_row = _log_sigmoid(gates_t[0:2 * SUBLANES, :])
    li_row = gates_t[0:M_HEADS, :]
    row_i = lax.broadcasted_iota(I32, (L, L), 0)
    col_i = lax.broadcasted_iota(I32, (L, L), 1)
    causal = col_i <= row_i
    lower = causal.astype(BF16)
    upper = (row_i <= col_i).astype(BF16)
    lf_ch, lf_cl = _split_bf16(lf_col)
    b_col_all = (jnp.dot(lower, lf_ch, preferred_element_type=F32)
                 + jnp.dot(lower, lf_cl, preferred_element_type=F32))
    lf_rh, lf_rl = _split_bf16(lf_row)
    b_row_all = (jnp.dot(lf_rh, upper, preferred_element_type=F32)
                 + jnp.dot(lf_rl, upper, preferred_element_type=F32))
    ones_col = (lax.broadcasted_iota(I32, (L, LANES), 1) == 0).astype(F32)

    for h in range(M_HEADS):
        q = qk[:, h * M_DQK:(h + 1) * M_DQK].astype(BF16)
        k = qk[:, N_QK // 2 + h * M_DQK:N_QK // 2 + (h + 1) * M_DQK] * (M_DQK ** -0.5)
        vaug = jnp.concatenate([v_ref[:, h * M_DV:(h + 1) * M_DV], ones_col], axis=1).astype(BF16)
        b_col = b_col_all[:, M_HEADS + h:M_HEADS + h + 1]
        li_col = gates[:, h:h + 1]
        b_row = b_row_all[M_HEADS + h:M_HEADS + h + 1, :]
        li_r = li_row[h:h + 1, :]
        m_prev = m_ref[h]
        m_prev1 = m_prev[:, 0:1]
        d = jnp.where(causal, b_col - b_row + li_r, -jnp.inf)
        inter = b_col + m_prev1
        m_t = jnp.maximum(inter, jnp.max(d, axis=-1, keepdims=True))
        s = lax.dot_general(q, k.astype(BF16), _NT, preferred_element_type=F32)
        w = jnp.exp(d - m_t) * s
        e_inter = jnp.exp(inter - m_t)
        c_aug = c_ref[h]
        num = (jnp.dot(w.astype(BF16), vaug, preferred_element_type=F32)
               + e_inter * jnp.dot(q, c_aug.astype(BF16), preferred_element_type=F32))
        den = num[:, M_DV:M_DV + 1]
        hh = num[:, :M_DV] / jnp.maximum(jnp.abs(den), jnp.exp(-m_t))
        mu = jnp.mean(hh, axis=-1, keepdims=True)
        dh = hh - mu
        var = jnp.mean(dh * dh, axis=-1, keepdims=True)
        hn = dh * lax.rsqrt(var + LN_EPS)
        sl = slice(h * M_DV, (h + 1) * M_DV)
        out_ref[:, sl] = hn * ng_ref[:, sl] * _sigmoid(o_ref[:, sl])

        b_last = b_col[L - 1:L, :]
        g_col = b_last - b_col + li_col
        m_new = jnp.maximum(b_last + m_prev1, jnp.max(g_col, axis=0, keepdims=True))
        decay = jnp.exp(b_last + m_prev1 - m_new)
        wk = (jnp.exp(g_col - m_new) * k).astype(BF16)
        c_ref[h] = decay * c_aug + lax.dot_general(wk, vaug, _TN, preferred_element_type=F32)
        m_ref[h] = jnp.broadcast_to(m_new, (1, LANES))


def _mlstm_core(qk, v, o, gates, conv_w, conv_b, norm_g, bn, seq):
    L = CHUNK
    nc = seq // L
    tok = lambda w: pl.BlockSpec((L, w), lambda b, c: (b * nc + c, 0))
    full = lambda r, w: pl.BlockSpec((r, w), lambda b, c: (0, 0))
    return pl.pallas_call(
        _mlstm_kernel,
        grid=(bn, nc),
        in_specs=[tok(N_QK), tok(N_V), tok(D_MODEL), tok(LANES),
                  full(M_CONV, N_QK), full(1, N_QK), full(1, N_V)],
        out_specs=tok(N_V),
        out_shape=jax.ShapeDtypeStruct((bn * seq, N_V), F32),
        scratch_shapes=[
            pltpu.VMEM((L + SUBLANES, N_QK), F32),
            pltpu.VMEM((M_HEADS, M_DQK, M_DV + LANES), F32),
            pltpu.VMEM((M_HEADS, 1, LANES), F32),
        ],
        compiler_params=_cparams(2),
        name="mlstm",
    )(qk, v, o, gates, conv_w, conv_b.reshape(1, N_QK), norm_g.reshape(1, N_V))


def _mlstm_layer(x, mod, w_in, conv_w, conv_b, b_if, norm_g, w_out, lng, lnb, bn, seq):
    sh, sc, gate = mod
    w_qk = w_in[:, :N_QK].astype(BF16)
    w_v = w_in[:, N_QK:N_QK + N_V].astype(BF16)
    w_o = w_in[:, N_QK + N_V:N_QK + N_V + D_MODEL].astype(BF16)
    w_g = jnp.zeros((D_MODEL, LANES), F32).at[:, :2 * M_HEADS].set(w_in[:, N_QK + N_V + D_MODEL:])
    b_g = jnp.zeros((1, LANES), F32).at[0, :2 * M_HEADS].set(b_if)
    z = lambda n: jnp.zeros((1, n), F32)
    qk, v, o, gates = _proj(x, sc, sh, [w_qk, w_v, w_o, w_g], [z(N_QK), z(N_V), z(D_MODEL), b_g],
                            [False, False, False, True], seq)
    hg = _mlstm_core(qk, v, o, gates, conv_w, conv_b, norm_g, bn, seq)
    return _lin_ln(hg, x, gate, w_out.astype(BF16), lng, lnb, seq)


def _s5_in_kernel(x_ref, sc_ref, sh_ref, wt_ref, o_ref):
    h = x_ref[...] * (1.0 + sc_ref[...]) + sh_ref[...]
    ut = lax.dot_general(wt_ref[...], h.astype(BF16), _NT, preferred_element_type=F32)
    for cl in range(o_ref.shape[1]):
        o_ref[:, cl, :, :] = ut[:, cl * LANES:(cl + 1) * LANES].reshape(D_MODEL // SUBLANES, SUBLANES, LANES)


def _s5_in(x, sc, sh, w_t, seq, tm=512):
    t, d = x.shape
    tpb = seq // tm
    nch = tm // LANES
    vec = pl.BlockSpec((None, 1, d), lambda i: (i // tpb, 0, 0))
    return pl.pallas_call(
        _s5_in_kernel,
        grid=(t // tm,),
        in_specs=[pl.BlockSpec((tm, d), lambda i: (i, 0)), vec, vec,
                  pl.BlockSpec((d, d), lambda i: (0, 0))],
        out_specs=pl.BlockSpec((d // SUBLANES, nch, SUBLANES, LANES), lambda i: (0, i, 0, 0)),
        out_shape=jax.ShapeDtypeStruct((d // SUBLANES, t // LANES, SUBLANES, LANES), F32),
        compiler_params=_cparams(1),
        name="s5_in",
    )(x, sc, sh, w_t)


def _cpow(lr, lim, dt, steps):
    mag = jnp.exp(lr * dt * steps)
    ang = lim * dt * steps
    return mag * jnp.cos(ang), mag * jnp.sin(ang)


def _s5_kernel(nc, u_ref, lamr_ref, lamc_ref, ldt_ref, crep_ref, cirep_ref, brt_ref, bit_ref,
               brtt_ref, bitt_ref, crt_ref, cit_ref, y_ref, acc_ref, kt_ref, r_ref, toe_ref, a_ref, xin_ref):
    L = CHUNK
    G = S5_GROUP
    P = S5_STATE
    nct = acc_ref.shape[1]

    def u_rows(i):
        return u_ref[i // SUBLANES, pl.ds(i % SUBLANES, nct, stride=SUBLANES), :]

    dt = jnp.exp(ldt_ref[...])
    lr_r, lim_r = lamr_ref[0:1, :], lamr_ref[1:2, :]
    lr_c, lim_c = lamc_ref[:, 0:1], lamc_ref[:, 1:2]

    a_re, a_im = _cpow(lr_r, lim_r, dt, 1.0)
    lam_sq = lr_r * lr_r + lim_r * lim_r
    t_re = ((a_re - 1.0) * lr_r + a_im * lim_r) / lam_sq
    t_im = (a_im * lr_r - (a_re - 1.0) * lim_r) / lam_sq

    bbt_re = t_re * brt_ref[...] - t_im * bit_ref[...]
    bbt_im = t_re * bit_ref[...] + t_im * brt_ref[...]
    bbtt_re = t_re * brtt_ref[...] - t_im * bitt_ref[...]
    bbtt_im = t_re * bitt_ref[...] + t_im * brtt_ref[...]
    cb_re = crep_ref[...] * bbtt_re - cirep_ref[...] * bbtt_im
    cb_im = crep_ref[...] * bbtt_im + cirep_ref[...] * bbtt_re

    lag = lax.broadcasted_iota(I32, (P, L), 1).astype(F32)
    p0_re, p0_im = _cpow(lr_c, lim_c, dt, lag)
    p1_re, p1_im = _cpow(lr_c, lim_c, dt, lag + 1.0)
    kt_ref[...] = _dot3(cb_re, p0_re) - _dot3(cb_im, p0_im)

    for o in range(G):
        cr_o = crt_ref[:, o:o + 1]
        ci_o = cit_ref[:, o:o + 1]
        cols = slice((o % 2) * L, (o % 2 + 1) * L)
        r_ref[o // 2, 0:P, cols] = (cr_o * p1_re - ci_o * p1_im).astype(BF16)
        r_ref[o // 2, P:2 * P, cols] = (-(cr_o * p1_im + ci_o * p1_re)).astype(BF16)

    for i in range(G):
        a_ref[:, i * L:(i + 1) * L] = u_rows(i).astype(BF16)

    back = (L - 1.0) - lax.broadcasted_iota(I32, (L, P), 0).astype(F32)
    ps_re, ps_im = _cpow(lr_r, lim_r, dt, back)
    for i in range(G):
        bi_re = bbt_re[i:i + 1, :]
        bi_im = bbt_im[i:i + 1, :]
        w_i = jnp.concatenate([ps_re * bi_re - ps_im * bi_im, ps_re * bi_im + ps_im * bi_re], axis=1)
        toe_ref[0, i * L:(i + 1) * L, 0:2 * P] = w_i.astype(BF16)
    e_acc = jnp.dot(a_ref[...], toe_ref[0, :, 0:2 * P], preferred_element_type=F32)

    chunk_in_seq = lax.broadcasted_iota(I32, (nct, 2 * P), 0) % nc
    x = e_acc
    dist = 1
    while dist < nc:
        d_re, d_im = _cpow(lr_r, lim_r, dt, float(L * dist))
        m1 = jnp.concatenate([d_re, d_re], axis=1)
        m2 = jnp.concatenate([-d_im, d_im], axis=1)
        shifted = jnp.where(chunk_in_seq >= dist, pltpu.roll(x, dist, 0), 0.0)
        x = x + shifted * m1 + pltpu.roll(shifted, P, 1) * m2
        dist *= 2
    x_in = jnp.where(chunk_in_seq >= 1, pltpu.roll(x, 1, 0), 0.0)

    xin_ref[...] = x_in.astype(BF16)

    srow = lax.broadcasted_iota(I32, (L, L), 0)
    tcol = lax.broadcasted_iota(I32, (L, L), 1)
    keep = tcol >= srow

    def toeplitz(o, i):
        k_row = jnp.broadcast_to(kt_ref[pl.ds(o * G + i, 1), :], (L, L))
        return jnp.where(keep, pltpu.roll(k_row, 0, 1, stride=1, stride_axis=0), 0.0).astype(BF16)

    def build(op, slot):
        for i in range(G):
            for oo in range(2):
                toe_ref[slot, i * L:(i + 1) * L, oo * L:(oo + 1) * L] = toeplitz(2 * op + oo, i)

    def emit(op, slot):
        acc_ref[op] = (jnp.dot(xin_ref[...], r_ref[op], preferred_element_type=F32)
                       + jnp.dot(a_ref[...], toe_ref[slot], preferred_element_type=F32))

    n_op = G // 2
    build(0, 0)

    def two_pairs(q, _):
        build(2 * q + 1, 1)
        emit(2 * q, 0)
        build(jnp.minimum(2 * q + 2, n_op - 1), 0)
        emit(2 * q + 1, 1)
        return 0

    lax.fori_loop(0, n_op // 2, two_pairs, 0)

    for o in range(G):
        y_ref[o // SUBLANES, pl.ds(o % SUBLANES, nct, stride=SUBLANES), :] = (
            acc_ref[o // 2, :, (o % 2) * L:(o % 2 + 1) * L])


def _s5_core(u4, lam_re, lam_im, log_dt, b_re, b_im, c_re, c_im, bn, seq):
    G, P, L = S5_GROUP, S5_STATE, CHUNK
    ng = S5_GROUPS
    nc = seq // L
    nct = bn * nc
    rt = G // SUBLANES
    u3 = u4.reshape(D_MODEL // SUBLANES, nct * SUBLANES, L)
    lam_r = jnp.stack([lam_re, lam_im], axis=1)
    lam_c = jnp.stack([lam_re, lam_im], axis=2)
    brt = jnp.swapaxes(b_re, 1, 2)
    bit = jnp.swapaxes(b_im, 1, 2)
    spec = lambda a: pl.BlockSpec((None,) + a.shape[1:], lambda g: (g,) + (0,) * (a.ndim - 1))
    params = [
        lam_r, lam_c, log_dt.reshape(ng, 1, 1),
        jnp.repeat(c_re, G, axis=1), jnp.repeat(c_im, G, axis=1),
        brt, bit,
        jnp.tile(brt, (1, G, 1)), jnp.tile(bit, (1, G, 1)),
        jnp.swapaxes(c_re, 1, 2), jnp.swapaxes(c_im, 1, 2),
    ]
    y3 = pl.pallas_call(
        functools.partial(_s5_kernel, nc),
        grid=(ng,),
        in_specs=[pl.BlockSpec((rt, nct * SUBLANES, L), lambda g: (g, 0, 0))] + [spec(a) for a in params],
        out_specs=pl.BlockSpec((rt, nct * SUBLANES, L), lambda g: (g, 0, 0)),
        out_shape=jax.ShapeDtypeStruct(u3.shape, F32),
        scratch_shapes=[
            pltpu.VMEM((G // 2, nct, 2 * L), F32),
            pltpu.VMEM((G * G, L), F32),
            pltpu.VMEM((G // 2, 2 * P, 2 * L), BF16),
            pltpu.VMEM((2, G * L, 2 * L), BF16),
            pltpu.VMEM((nct, G * L), BF16),
            pltpu.VMEM((nct, 2 * P), BF16),
        ],
        compiler_params=_cparams(1),
        name="s5_core",
    )(u3, *params)
    return u3, y3


def _s5_out_kernel(u_ref, y_ref, dsk_ref, x_ref, g_ref, w_ref, lng_ref, lnb_ref, o_ref):
    nch = u_ref.shape[1] // SUBLANES
    for cl in range(nch):
        rows = slice(cl * SUBLANES, (cl + 1) * SUBLANES)
        v = _gelu_tanh(y_ref[:, rows, :] + dsk_ref[...] * u_ref[:, rows, :])
        vt = v.reshape(D_MODEL, LANES).T
        z = jnp.dot(vt.astype(BF16), w_ref[...], preferred_element_type=F32)
        y = z[:, :D_MODEL] * _sigmoid(z[:, D_MODEL:])
        tok = slice(cl * LANES, (cl + 1) * LANES)
        o_ref[tok, :] = _layer_norm(DN_ALPHA * x_ref[tok, :] + g_ref[...] * y, lng_ref[...], lnb_ref[...])


def _s5_out(u3, y3, d_skip, x, gate, w_glu, lng, lnb, seq, tm=512):
    t, d = x.shape
    tpb = seq // tm
    nch = tm // LANES
    dsk = jnp.broadcast_to(d_skip.reshape(d // SUBLANES, SUBLANES, 1), (d // SUBLANES, SUBLANES, LANES))
    row = pl.BlockSpec((1, d), lambda i: (0, 0))
    tile3 = pl.BlockSpec((d // SUBLANES, nch * SUBLANES, LANES), lambda i: (0, i, 0))
    return pl.pallas_call(
        _s5_out_kernel,
        grid=(t // tm,),
        in_specs=[tile3, tile3,
                  pl.BlockSpec(dsk.shape, lambda i: (0, 0, 0)),
                  pl.BlockSpec((tm, d), lambda i: (i, 0)),
                  pl.BlockSpec((None, 1, d), lambda i: (i // tpb, 0, 0)),
                  pl.BlockSpec(w_glu.shape, lambda i: (0, 0)),
                  row, row],
        out_specs=pl.BlockSpec((tm, d), lambda i: (i, 0)),
        out_shape=jax.ShapeDtypeStruct((t, d), F32),
        compiler_params=_cparams(1),
        name="s5_out",
    )(u3, y3, dsk, x, gate, w_glu, lng.reshape(1, d), lnb.reshape(1, d))


def _s5_layer(x, mod, w_in, lam_re, lam_im, log_dt, b_re, b_im, c_re, c_im, d_skip, w_glu, lng, lnb, bn, seq):
    sh, sc, gate = mod
    u4 = _s5_in(x, sc, sh, w_in.T.astype(BF16), seq)
    u3, y3 = _s5_core(u4, lam_re, lam_im, log_dt, b_re, b_im, c_re, c_im, bn, seq)
    return _s5_out(u3, y3, d_skip, x, gate, w_glu.astype(BF16), lng, lnb, seq)


def _swa_kernel(sink_ref, q_ref, kvc_ref, kvp_ref, o_ref):
    L = WINDOW
    n = pl.program_id(1)
    qi = lax.broadcasted_iota(I32, (L, 2 * L), 0)
    kj = lax.broadcasted_iota(I32, (L, 2 * L), 1)
    kmin = jnp.where(n > 0, 0, L)
    valid = (kj > qi) & (kj <= qi + L) & (kj >= kmin)
    nkv = A_KV_HEADS * A_HEAD_DIM
    for h in range(A_KV_HEADS):
        ks = slice(h * A_HEAD_DIM, (h + 1) * A_HEAD_DIM)
        vs = slice(nkv + h * A_HEAD_DIM, nkv + (h + 1) * A_HEAD_DIM)
        kb = jnp.concatenate([kvp_ref[:, ks], kvc_ref[:, ks]], axis=0).astype(BF16)
        vb = jnp.concatenate([kvp_ref[:, vs], kvc_ref[:, vs]], axis=0).astype(BF16)
        for g in range(A_GROUP):
            hd = h * A_GROUP + g
            cs = slice(hd * A_HEAD_DIM, (hd + 1) * A_HEAD_DIM)
            q = (q_ref[:, cs] * (A_HEAD_DIM ** -0.5)).astype(BF16)
            s = lax.dot_general(q, kb, _NT, preferred_element_type=F32)
            s = jnp.where(valid, s, -jnp.inf)
            sink = sink_ref[hd]
            m = jnp.maximum(jnp.max(s, axis=-1, keepdims=True), sink)
            p = jnp.exp(s - m)
            den = jnp.sum(p, axis=-1, keepdims=True) + jnp.exp(sink - m)
            o_ref[:, cs] = jnp.dot(p.astype(BF16), vb, preferred_element_type=F32) / den


def _swa_core(qkv, sinks, bn, seq):
    L = WINDOW
    nb = seq // L
    nq = A_HEADS * A_HEAD_DIM
    nkv2 = 2 * A_KV_HEADS * A_HEAD_DIM
    kvcol = nq // nkv2
    return pl.pallas_call(
        _swa_kernel,
        grid_spec=pltpu.PrefetchScalarGridSpec(
            num_scalar_prefetch=1,
            grid=(bn, nb),
            in_specs=[
                pl.BlockSpec((L, nq), lambda b, n, s: (b * nb + n, 0)),
                pl.BlockSpec((L, nkv2), lambda b, n, s: (b * nb + n, kvcol)),
                pl.BlockSpec((L, nkv2), lambda b, n, s: (b * nb + jnp.maximum(n - 1, 0), kvcol)),
            ],
            out_specs=pl.BlockSpec((L, nq), lambda b, n, s: (b * nb + n, 0)),
        ),
        out_shape=jax.ShapeDtypeStruct((bn * seq, nq), F32),
        compiler_params=_cparams(2),
        name="swa",
    )(sinks, qkv, qkv, qkv)


def _swa_layer(x, mod, w_qkv, b_qkv, sinks, w_o, lng, lnb, bn, seq):
    sh, sc, gate = mod
    (qkv,) = _proj(x, sc, sh, [w_qkv.astype(BF16)], [b_qkv.reshape(1, -1)], [False], seq)
    o = _swa_core(qkv, sinks, bn, seq)
    return _lin_ln(o, x, gate, w_o.astype(BF16), lng, lnb, seq)


ROUTER_OFF = SUBLANES


def _router_kernel(x_ref, sc_ref, sh_ref, w_ref, b_ref, e_ref, rank_ref, wcol_ref, cnt_ref, carry_ref):
    tm = x_ref.shape[0]
    i = pl.program_id(0)

    @pl.when(i == 0)
    def _():
        carry_ref[...] = jnp.zeros_like(carry_ref)

    h = x_ref[...] * (1.0 + sc_ref[...]) + sh_ref[...]
    logits = _dot3(h, w_ref[...]) + b_ref[...]
    lt = logits.T
    gl = [lt[k:k + 1, :] for k in range(E_GROUPS)]
    gmax = jnp.maximum(jnp.maximum(gl[0], gl[1]), jnp.maximum(gl[2], gl[3]))
    g_sel = jnp.where(gl[0] == gmax, 0, jnp.where(gl[1] == gmax, 1, jnp.where(gl[2] == gmax, 2, 3)))
    p_g = 1.0 / (jnp.exp(gl[0] - gmax) + jnp.exp(gl[1] - gmax) + jnp.exp(gl[2] - gmax) + jnp.exp(gl[3] - gmax))
    el = [lt[ROUTER_OFF + k * E_PER_GROUP:ROUTER_OFF + (k + 1) * E_PER_GROUP, :] for k in range(E_GROUPS)]
    eg = jnp.where(g_sel == 0, el[0], jnp.where(g_sel == 1, el[1], jnp.where(g_sel == 2, el[2], el[3])))
    sub = lax.broadcasted_iota(I32, (E_PER_GROUP, tm), 0).astype(F32)
    v1 = jnp.max(eg, axis=0, keepdims=True)
    i1 = jnp.min(jnp.where(eg == v1, sub, float(E_PER_GROUP)), axis=0, keepdims=True)
    eg2 = jnp.where(sub == i1, -jnp.inf, eg)
    v2 = jnp.max(eg2, axis=0, keepdims=True)
    i2 = jnp.min(jnp.where(eg2 == v2, sub, float(E_PER_GROUP)), axis=0, keepdims=True)
    t2 = jnp.exp(v2 - v1)
    w1 = p_g / (1.0 + t2)
    w2 = w1 * t2
    e1 = g_sel * E_PER_GROUP + i1.astype(I32)
    e2 = g_sel * E_PER_GROUP + i2.astype(I32)

    sub8 = lax.broadcasted_iota(I32, (SUBLANES, tm), 0)
    e_ref[...] = jnp.where(sub8 == 0, e1, jnp.where(sub8 == 1, e2, 0))
    subl = lax.broadcasted_iota(I32, (LANES, tm), 0)
    wcol_ref[...] = jnp.where(subl == 0, w1, jnp.where(subl == 1, w2, 0.0)).T

    eid = lax.broadcasted_iota(I32, (N_EXPERTS, tm), 0)
    oh1 = eid == e1
    oh2 = eid == e2
    ohs = jnp.where(oh1 | oh2, 1.0, 0.0)
    upper = (lax.broadcasted_iota(I32, (tm, tm), 0) <= lax.broadcasted_iota(I32, (tm, tm), 1)).astype(BF16)
    incl = jnp.dot(ohs.astype(BF16), upper, preferred_element_type=F32)
    base = incl - ohs + carry_ref[:, 0:1]
    r1 = jnp.sum(jnp.where(oh1, base, 0.0), axis=0, keepdims=True)
    r2 = jnp.sum(jnp.where(oh2, base, 0.0), axis=0, keepdims=True)
    rank_ref[...] = jnp.where(sub8 == 0, r1, jnp.where(sub8 == 1, r2, 0.0)).astype(I32)
    carry_ref[...] = carry_ref[...] + jnp.sum(ohs, axis=1, keepdims=True)
    cnt_ref[...] = carry_ref[...]


def _router(x, sc, sh, w_group, b_group, w_router, b_router, seq, tm=512):
    t, d = x.shape
    tpb = seq // tm
    w = jnp.zeros((d, LANES), F32).at[:, :E_GROUPS].set(w_group).at[:, ROUTER_OFF:ROUTER_OFF + N_EXPERTS].set(w_router)
    b = jnp.zeros((1, LANES), F32).at[0, :E_GROUPS].set(b_group).at[0, ROUTER_OFF:ROUTER_OFF + N_EXPERTS].set(b_router)
    vec = pl.BlockSpec((None, 1, d), lambda i: (i // tpb, 0, 0))
    rows = pl.BlockSpec((SUBLANES, tm), lambda i: (0, i))
    return pl.pallas_call(
        _router_kernel,
        grid=(t // tm,),
        in_specs=[pl.BlockSpec((tm, d), lambda i: (i, 0)), vec, vec,
                  pl.BlockSpec((d, LANES), lambda i: (0, 0)), pl.BlockSpec((1, LANES), lambda i: (0, 0))],
        out_specs=[rows, rows, pl.BlockSpec((tm, LANES), lambda i: (i, 0)),
                   pl.BlockSpec((N_EXPERTS, LANES), lambda i: (0, 0))],
        out_shape=[jax.ShapeDtypeStruct((SUBLANES, t), I32), jax.ShapeDtypeStruct((SUBLANES, t), I32),
                   jax.ShapeDtypeStruct((t, LANES), F32), jax.ShapeDtypeStruct((N_EXPERTS, LANES), F32)],
        scratch_shapes=[pltpu.VMEM((N_EXPERTS, LANES), F32)],
        compiler_params=_cparams(1),
        name="router",
    )(x, sc, sh, w, b)


def _pos_kernel(e_ref, rank_ref, pstart_ref, pos_ref):
    tm = e_ref.shape[1]
    eid = lax.broadcasted_iota(I32, (N_EXPERTS, tm), 0)
    start = pstart_ref[:, 0:1]
    rows = []
    for k in range(TOP_K):
        base = jnp.sum(jnp.where(eid == e_ref[k:k + 1, :], start, 0.0), axis=0, keepdims=True)
        rows.append(base.astype(I32) + rank_ref[k:k + 1, :])
    sub8 = lax.broadcasted_iota(I32, (SUBLANES, tm), 0)
    pos_ref[...] = jnp.where(sub8 == 0, rows[0], jnp.where(sub8 == 1, rows[1], 0))


def _positions(e, rank, pstart, tm=2048):
    t = e.shape[1]
    tm = min(tm, t)
    rows = pl.BlockSpec((SUBLANES, tm), lambda i: (0, i))
    return pl.pallas_call(
        _pos_kernel,
        grid=(t // tm,),
        in_specs=[rows, rows, pl.BlockSpec((N_EXPERTS, LANES), lambda i: (0, 0))],
        out_specs=rows,
        out_shape=jax.ShapeDtypeStruct((SUBLANES, t), I32),
        compiler_params=_cparams(1),
        name="moe_pos",
    )(e, rank, jnp.broadcast_to(pstart.astype(F32)[:, None], (N_EXPERTS, LANES)))


SLAB = D_MODEL // LANES


def _to_slabs(ref, val):
    rows = val.shape[0]
    for cidx in range(SLAB):
        ref[pl.ds(cidx, rows, stride=SLAB), :] = val[:, cidx * LANES:(cidx + 1) * LANES]


def _from_slabs(ref, rows):
    return jnp.concatenate([ref[pl.ds(cidx, rows, stride=SLAB), :] for cidx in range(SLAB)], axis=1)


def _row_copy(src, src_row, dst, dst_row, sem):
    s0 = pl.multiple_of(src_row * SLAB, SLAB)
    d0 = pl.multiple_of(dst_row * SLAB, SLAB)
    return pltpu.make_async_copy(src.at[pl.ds(s0, SLAB), :], dst.at[pl.ds(d0, SLAB), :], sem)


def _dispatch_kernel(pend_ref, x_ref, sc_ref, sh_ref, pos_ref, buf_ref, hbuf, zbuf, sem, zsem):
    tm = x_ref.shape[0]
    i = pl.program_id(0)

    nt = pl.num_programs(0)
    slot = i % 2

    def zero_block(j):
        start = pl.multiple_of(j * (MOE_BM * SLAB), MOE_BM * SLAB)
        return pltpu.make_async_copy(zbuf, buf_ref.at[pl.ds(start, MOE_BM * SLAB), :], zsem)

    @pl.when(i == 0)
    def _():
        zbuf[...] = jnp.zeros_like(zbuf)

        def pad_block(e, go):
            end = pend_ref[e] // MOE_BM
            prev = jnp.where(e > 0, pend_ref[jnp.maximum(e - 1, 0)] // MOE_BM, 0)

            @pl.when(end > prev)
            def _():
                cp = zero_block(end - 1)
                cp.start() if go else cp.wait()
            return 0

        lax.fori_loop(0, N_EXPERTS, lambda e, c: pad_block(e, True), 0)
        lax.fori_loop(0, N_EXPERTS, lambda e, c: pad_block(e, False), 0)
        first_unused = pend_ref[N_EXPERTS - 1] // MOE_BM
        n_blocks = buf_ref.shape[0] // (MOE_BM * SLAB)
        lax.fori_loop(first_unused, n_blocks, lambda j, c: (zero_block(j).start(), c)[1], 0)
        lax.fori_loop(first_unused, n_blocks, lambda j, c: (zero_block(j).wait(), c)[1], 0)

    _to_slabs(hbuf.at[slot], x_ref[...] * (1.0 + sc_ref[...]) + sh_ref[...])

    def issue(rb, _):
        for u in range(SUBLANES):
            r = rb * SUBLANES + u
            for k in range(TOP_K):
                _row_copy(hbuf.at[slot], r, buf_ref, pos_ref[k, r], sem.at[slot]).start(priority=k)
        return 0

    lax.fori_loop(0, tm // SUBLANES, issue, 0)

    def drain(s):
        for k in range(TOP_K):
            pltpu.make_async_copy(hbuf.at[s], buf_ref.at[pl.ds(0, tm * SLAB), :], sem.at[s]).wait()

    @pl.when(i > 0)
    def _():
        drain(1 - slot)

    @pl.when(i == nt - 1)
    def _():
        drain(slot)


def _dispatch(x, sc, sh, pos, pend, n_rows, seq, tm=256):
    t, d = x.shape
    tpb = seq // tm
    nt = t // tm
    pos3 = pos[:TOP_K].reshape(TOP_K, nt, tm).transpose(1, 0, 2)
    vec = pl.BlockSpec((None, 1, d), lambda i, p: (i // tpb, 0, 0))
    return pl.pallas_call(
        _dispatch_kernel,
        grid_spec=pltpu.PrefetchScalarGridSpec(
            num_scalar_prefetch=1,
            grid=(nt,),
            in_specs=[pl.BlockSpec((tm, d), lambda i, p: (i, 0)), vec, vec,
                      pl.BlockSpec((None, TOP_K, tm), lambda i, p: (i, 0, 0), memory_space=pltpu.SMEM)],
            out_specs=pl.BlockSpec(memory_space=pl.ANY),
            scratch_shapes=[pltpu.VMEM((2, tm * SLAB, LANES), F32), pltpu.VMEM((MOE_BM * SLAB, LANES), F32),
                            pltpu.SemaphoreType.DMA((2,)), pltpu.SemaphoreType.DMA(())],
        ),
        out_shape=jax.ShapeDtypeStruct((n_rows * SLAB, LANES), F32),
        compiler_params=_cparams(1),
        name="moe_dispatch",
    )(pend, x, sc, sh, pos3)


def _experts_kernel(blk_e_ref, nb_ref, x_ref, w1_ref, w3_ref, w2_ref, y_ref):
    @pl.when(pl.program_id(0) < nb_ref[0])
    def _():
        xb = _from_slabs(x_ref, MOE_BM).astype(BF16)
        h1 = jnp.dot(xb, w1_ref[...], preferred_element_type=F32)
        h3 = jnp.dot(xb, w3_ref[...], preferred_element_type=F32)
        a = (_silu(h1) * h3).astype(BF16)
        _to_slabs(y_ref, jnp.dot(a, w2_ref[...], preferred_element_type=F32))

    @pl.when(pl.program_id(0) >= nb_ref[0])
    def _():
        y_ref[...] = jnp.zeros_like(y_ref)


def _experts(buf, blk_e, nb_used, w1, w3, w2):
    nblk = buf.shape[0] // (MOE_BM * SLAB)
    d, de = w1.shape[1], w1.shape[2]
    row_blk = lambda j, be, nb: (jnp.minimum(j, nb[0] - 1), 0)
    wsel = lambda j, be, nb: (be[j], 0, 0)
    return pl.pallas_call(
        _experts_kernel,
        grid_spec=pltpu.PrefetchScalarGridSpec(
            num_scalar_prefetch=2,
            grid=(nblk,),
            in_specs=[pl.BlockSpec((MOE_BM * SLAB, LANES), row_blk),
                      pl.BlockSpec((None, d, de), wsel), pl.BlockSpec((None, d, de), wsel),
                      pl.BlockSpec((None, de, d), wsel)],
            out_specs=pl.BlockSpec((MOE_BM * SLAB, LANES), lambda j, be, nb: (j, 0)),
        ),
        out_shape=jax.ShapeDtypeStruct(buf.shape, F32),
        compiler_params=_cparams(1),
        name="moe_experts",
    )(blk_e, nb_used, buf, w1, w3, w2)


def _combine_kernel(pos_ref, posn_ref, y_ref, wcol_ref, x_ref, g_ref, lng_ref, lnb_ref, o_ref, ybuf, sem):
    tm = x_ref.shape[0]
    i = pl.program_id(0)
    nt = pl.num_programs(0)
    slot = i % 2

    def gather(p_ref, s):
        def issue(rb, _):
            for u in range(SUBLANES):
                r = rb * SUBLANES + u
                for k in range(TOP_K):
                    _row_copy(y_ref, p_ref[k, r], ybuf.at[s, k], r, sem.at[s]).start(priority=k)
            return 0

        lax.fori_loop(0, tm // SUBLANES, issue, 0)

    @pl.when(i == 0)
    def _():
        gather(pos_ref, 0)

    @pl.when(i + 1 < nt)
    def _():
        gather(posn_ref, 1 - slot)

    for k in range(TOP_K):
        pltpu.make_async_copy(y_ref.at[pl.ds(0, tm * SLAB), :], ybuf.at[slot, k], sem.at[slot]).wait()
    y = (wcol_ref[:, 0:1] * _from_slabs(ybuf.at[slot, 0], tm)
         + wcol_ref[:, 1:2] * _from_slabs(ybuf.at[slot, 1], tm))
    o_ref[...] = _layer_norm(DN_ALPHA * x_ref[...] + g_ref[...] * y, lng_ref[...], lnb_ref[...])


def _combine(y, pos, wcol, x, gate, lng, lnb, seq, tm=256):
    t, d = x.shape
    tpb = seq // tm
    nt = t // tm
    pos3 = pos[:TOP_K].reshape(TOP_K, nt, tm).transpose(1, 0, 2)
    row = pl.BlockSpec((1, d), lambda i: (0, 0))
    return pl.pallas_call(
        _combine_kernel,
        grid=(nt,),
        in_specs=[pl.BlockSpec((None, TOP_K, tm), lambda i: (i, 0, 0), memory_space=pltpu.SMEM),
                  pl.BlockSpec((None, TOP_K, tm), lambda i: (jnp.minimum(i + 1, nt - 1), 0, 0),
                               memory_space=pltpu.SMEM),
                  pl.BlockSpec(memory_space=pl.ANY),
                  pl.BlockSpec((tm, LANES), lambda i: (i, 0)),
                  pl.BlockSpec((tm, d), lambda i: (i, 0)),
                  pl.BlockSpec((None, 1, d), lambda i: (i // tpb, 0, 0)),
                  row, row],
        out_specs=pl.BlockSpec((tm, d), lambda i: (i, 0)),
        out_shape=jax.ShapeDtypeStruct((t, d), F32),
        scratch_shapes=[pltpu.VMEM((2, TOP_K, tm * SLAB, LANES), F32), pltpu.SemaphoreType.DMA((2,))],
        compiler_params=_cparams(1),
        name="moe_combine",
    )(pos3, pos3, y, wcol, x, gate, lng.reshape(1, d), lnb.reshape(1, d))


def _moe_layer(x, mod, w_group, b_group, w_router, b_router, w1, w3, w2, lng, lnb, seq):
    sh, sc, gate = mod
    t, d = x.shape
    e, rank, wcol, cnt = _router(x, sc, sh, w_group, b_group, w_router, b_router, seq)
    counts = cnt[:, 0].astype(I32)
    padded = (counts + MOE_BM - 1) // MOE_BM * MOE_BM
    pend = jnp.cumsum(padded)
    pstart = pend - padded
    n_rows = t * TOP_K + N_EXPERTS * MOE_BM
    nblk = n_rows // MOE_BM
    blk_start = jnp.arange(nblk, dtype=I32) * MOE_BM
    blk_e = jnp.minimum(jnp.sum(pend[None, :] <= blk_start[:, None], axis=1), N_EXPERTS - 1).astype(I32)
    nb_used = (pend[-1:] // MOE_BM).astype(I32)
    pos = _positions(e, rank, pstart)
    buf = _dispatch(x, sc, sh, pos, pend.astype(I32), n_rows, seq)
    y = _experts(buf, blk_e, nb_used, w1.astype(BF16), w3.astype(BF16), w2.astype(BF16))
    return _combine(y, pos, wcol, x, gate, lng, lnb, seq)


def kernel(x, c, ada_w, ada_b, ln1_g, ln1_b, ln2_g, ln2_b, mlstm_w_in, mlstm_conv_w, mlstm_conv_b, mlstm_b_if, mlstm_norm_g, mlstm_w_out, s5_w_in, s5_lam_re, s5_lam_im, s5_log_dt, s5_b_re, s5_b_im, s5_c_re, s5_c_im, s5_d, s5_w_glu, swa_w_qkv, swa_b_qkv, swa_sinks, swa_w_o, moe_w_group, moe_b_group, moe_w_router, moe_b_router, moe_w1, moe_w3, moe_w2):
    bn, seq, d = x.shape
    depth = ada_w.shape[0]
    mods = _ada(c, ada_w, ada_b).reshape(depth, bn, 6, 1, d)
    xt = x.reshape(bn * seq, d)
    for i in range(depth):
        m1 = tuple(mods[i, :, k] for k in range(3))
        m2 = tuple(mods[i, :, k] for k in range(3, 6))
        kind, j = i % N_MIXERS, i // N_MIXERS
        if kind == 0:
            xt = _mlstm_layer(xt, m1, mlstm_w_in[j], mlstm_conv_w[j], mlstm_conv_b[j], mlstm_b_if[j],
                              mlstm_norm_g[j], mlstm_w_out[j], ln1_g[i], ln1_b[i], bn, seq)
        elif kind == 1:
            xt = _s5_layer(xt, m1, s5_w_in[j], s5_lam_re[j], s5_lam_im[j], s5_log_dt[j], s5_b_re[j], s5_b_im[j],
                           s5_c_re[j], s5_c_im[j], s5_d[j], s5_w_glu[j], ln1_g[i], ln1_b[i], bn, seq)
        else:
            xt = _swa_layer(xt, m1, swa_w_qkv[j], swa_b_qkv[j], swa_sinks[j], swa_w_o[j],
                            ln1_g[i], ln1_b[i], bn, seq)
        xt = _moe_layer(xt, m2, moe_w_group[i], moe_b_group[i], moe_w_router[i], moe_b_router[i],
                        moe_w1[i], moe_w3[i], moe_w2[i], ln2_g[i], ln2_b[i], seq)
    return xt.reshape(bn, seq, d)
```

```python
import functools
import math

import jax
import jax.numpy as jnp
from jax import lax
from jax.experimental import pallas as pl
from jax.experimental.pallas import tpu as pltpu

F32 = jnp.float32
BF16 = jnp.bfloat16
I32 = jnp.int32

D_MODEL = 1024
DEPTH = 4
N_MIXERS = 3
DN_ALPHA = (2 * DEPTH) ** 0.25
LN_EPS = 1e-5

M_HEADS = 4
M_DQK = D_MODEL // 8
M_DV = D_MODEL // M_HEADS
M_CONV = 4
N_QK = 2 * M_HEADS * M_DQK
N_V = M_HEADS * M_DV

S5_GROUP = 16
S5_GROUPS = D_MODEL // S5_GROUP
S5_STATE = 64

A_HEADS = 16
A_KV_HEADS = 4
A_GROUP = A_HEADS // A_KV_HEADS
A_HEAD_DIM = D_MODEL // A_HEADS
WINDOW = 128

E_GROUPS = 4
E_PER_GROUP = 8
N_EXPERTS = E_GROUPS * E_PER_GROUP
TOP_K = 2
D_EXPERT = 512

LANES = 128
SUBLANES = 8
CHUNK = LANES
MOE_BM = 512
VMEM_LIMIT = 56 * 1024 * 1024


def _cparams(n_axes):
    return pltpu.CompilerParams(dimension_semantics=("arbitrary",) * n_axes, vmem_limit_bytes=VMEM_LIMIT)


def _split_bf16(a):
    hi = a.astype(BF16)
    lo = (a - hi.astype(F32)).astype(BF16)
    return hi, lo


def _dot3(a, b, dims=None):
    ah, al = _split_bf16(a)
    bh, bl = _split_bf16(b)
    if dims is None:
        dot = lambda x, y: jnp.dot(x, y, preferred_element_type=F32)
    else:
        dot = lambda x, y: lax.dot_general(x, y, dims, preferred_element_type=F32)
    return dot(ah, bh) + (dot(ah, bl) + dot(al, bh))


def _bdot(a, b):
    return jnp.dot(a.astype(BF16), b.astype(BF16), preferred_element_type=F32)


_NT = (((1,), (1,)), ((), ()))
_TN = (((0,), (0,)), ((), ()))


def _layer_norm(v, g, b):
    mu = jnp.mean(v, axis=-1, keepdims=True)
    d = v - mu
    var = jnp.mean(d * d, axis=-1, keepdims=True)
    return d * lax.rsqrt(var + LN_EPS) * g + b


def _sigmoid(x):
    return 1.0 / (1.0 + jnp.exp(-x))


def _silu(x):
    return x * _sigmoid(x)


def _log_sigmoid(x):
    return jnp.minimum(x, 0.0) - jnp.log1p(jnp.exp(-jnp.abs(x)))


def _gelu_tanh(x):
    return 0.5 * x * (1.0 + jnp.tanh(math.sqrt(2.0 / math.pi) * (x + 0.044715 * (x * x * x))))


def _ada_kernel(c_ref, w_ref, b_ref, o_ref):
    cond = _silu(c_ref[...])
    o_ref[...] = _dot3(cond, w_ref[...]) + b_ref[...]


def _ada(c, ada_w, ada_b):
    bn, d = c.shape
    depth, _, n = ada_w.shape
    tn = 1536
    c8 = jnp.zeros((SUBLANES, d), F32).at[:bn].set(c)
    out = pl.pallas_call(
        _ada_kernel,
        grid=(depth, n // tn),
        in_specs=[
            pl.BlockSpec((SUBLANES, d), lambda l, j: (0, 0)),
            pl.BlockSpec((None, d, tn), lambda l, j: (l, 0, j)),
            pl.BlockSpec((None, 1, tn), lambda l, j: (l, 0, j)),
        ],
        out_specs=pl.BlockSpec((None, SUBLANES, tn), lambda l, j: (l, 0, j)),
        out_shape=jax.ShapeDtypeStruct((depth, SUBLANES, n), F32),
        compiler_params=_cparams(2),
        name="ada",
    )(c8, ada_w, ada_b.reshape(depth, 1, n))
    return out[:, :bn]


def _proj_kernel(precise, x_ref, sc_ref, sh_ref, *refs):
    n = len(precise)
    w_refs, b_refs, o_refs = refs[:n], refs[n:2 * n], refs[2 * n:]
    h = x_ref[...] * (1.0 + sc_ref[...]) + sh_ref[...]
    hb = h.astype(BF16)
    for w_ref, b_ref, o_ref, p in zip(w_refs, b_refs, o_refs, precise):
        if p:
            r = _dot3(h, w_ref[...])
        else:
            r = jnp.dot(hb, w_ref[...], preferred_element_type=F32)
        o_ref[...] = r + b_ref[...]


def _proj(x, sc, sh, ws, bs, precise, seq, tm=512):
    t, d = x.shape
    tpb = seq // tm
    vec = pl.BlockSpec((None, 1, d), lambda i: (i // tpb, 0, 0))
    in_specs = [pl.BlockSpec((tm, d), lambda i: (i, 0)), vec, vec]
    in_specs += [pl.BlockSpec(w.shape, lambda i: (0, 0)) for w in ws]
    in_specs += [pl.BlockSpec(b.shape, lambda i: (0, 0)) for b in bs]
    return pl.pallas_call(
        functools.partial(_proj_kernel, tuple(precise)),
        grid=(t // tm,),
        in_specs=in_specs,
        out_specs=[pl.BlockSpec((tm, w.shape[1]), lambda i: (i, 0)) for w in ws],
        out_shape=[jax.ShapeDtypeStruct((t, w.shape[1]), F32) for w in ws],
        compiler_params=_cparams(1),
        name="proj",
    )(x, sc, sh, *ws, *bs)


def _lin_ln_kernel(a_ref, x_ref, g_ref, w_ref, lng_ref, lnb_ref, o_ref):
    y = jnp.dot(a_ref[...].astype(BF16), w_ref[...], preferred_element_type=F32)
    o_ref[...] = _layer_norm(DN_ALPHA * x_ref[...] + g_ref[...] * y, lng_ref[...], lnb_ref[...])


def _lin_ln(a, x, gate, w, lng, lnb, seq, tm=512):
    t, d = x.shape
    k = a.shape[1]
    tpb = seq // tm
    row = pl.BlockSpec((1, d), lambda i: (0, 0))
    return pl.pallas_call(
        _lin_ln_kernel,
        grid=(t // tm,),
        in_specs=[
            pl.BlockSpec((tm, k), lambda i: (i, 0)),
            pl.BlockSpec((tm, d), lambda i: (i, 0)),
            pl.BlockSpec((None, 1, d), lambda i: (i // tpb, 0, 0)),
            pl.BlockSpec((k, d), lambda i: (0, 0)),
            row, row,
        ],
        out_specs=pl.BlockSpec((tm, d), lambda i: (i, 0)),
        out_shape=jax.ShapeDtypeStruct((t, d), F32),
        compiler_params=_cparams(1),
        name="lin_ln",
    )(a, x, gate, w, lng.reshape(1, d), lnb.reshape(1, d))


def _mlstm_kernel(qk_ref, v_ref, o_ref, gates_ref, cw_ref, cb_ref, ng_ref, out_ref,
                  qkbuf, c_ref, m_ref):
    @pl.when(pl.program_id(0) == 0)
    def _():
        qkbuf[...] = jnp.zeros_like(qkbuf)
        c_ref[...] = jnp.zeros_like(c_ref)
        m_ref[...] = jnp.zeros_like(m_ref)

    for b in range(qk_ref.shape[0]):
        _mlstm_chunk(qk_ref.at[b], v_ref.at[b], o_ref.at[b], gates_ref.at[b], cw_ref, cb_ref, ng_ref,
                     out_ref.at[b], qkbuf.at[b], c_ref.at[b], m_ref.at[b])


def _mlstm_chunk(qk_ref, v_ref, o_ref, gates_ref, cw_ref, cb_ref, ng_ref, out_ref, qkbuf, c_ref, m_ref):
    L = CHUNK
    qkbuf[SUBLANES:SUBLANES + L, :] = qk_ref[...]
    conv = cb_ref[...] + cw_ref[0:1, :] * qkbuf[pl.ds(SUBLANES - 3, L), :]
    for j in range(1, M_CONV):
        conv = conv + cw_ref[j:j + 1, :] * qkbuf[pl.ds(SUBLANES - 3 + j, L), :]
    qk = _silu(conv)
    qkbuf[0:SUBLANES, :] = qkbuf[L:L + SUBLANES, :]

    gates = gates_ref[...]
    gates_t = gates.T
    lf_col = _log_sigmoid(gates)
    lf_row = _log_sigmoid(gates_t[0:2 * SUBLANES, :])
    li_row = gates_t[0:M_HEADS, :]
    row_i = lax.broadcasted_iota(I32, (L, L), 0)
    col_i = lax.broadcasted_iota(I32, (L, L), 1)
    causal = col_i <= row_i
    lower = causal.astype(BF16)
    upper = (row_i <= col_i).astype(BF16)
    lf_ch, lf_cl = _split_bf16(lf_col)
    b_col_all = (jnp.dot(lower, lf_ch, preferred_element_type=F32)
                 + jnp.dot(lower, lf_cl, preferred_element_type=F32))
    lf_rh, lf_rl = _split_bf16(lf_row)
    b_row_all = (jnp.dot(lf_rh, upper, preferred_element_type=F32)
                 + jnp.dot(lf_rl, upper, preferred_element_type=F32))
    ones_col = (lax.broadcasted_iota(I32, (L, LANES), 1) == 0).astype(F32)

    for h in range(M_HEADS):
        q = qk[:, h * M_DQK:(h + 1) * M_DQK].astype(BF16)
        k = qk[:, N_QK // 2 + h * M_DQK:N_QK // 2 + (h + 1) * M_DQK] * (M_DQK ** -0.5)
        vaug = jnp.concatenate([v_ref[:, h * M_DV:(h + 1) * M_DV], ones_col], axis=1).astype(BF16)
        b_col = b_col_all[:, M_HEADS + h:M_HEADS + h + 1]
        li_col = gates[:, h:h + 1]
        b_row = b_row_all[M_HEADS + h:M_HEADS + h + 1, :]
        li_r = li_row[h:h + 1, :]
        m_prev = m_ref[h]
        m_prev1 = m_prev[:, 0:1]
        d = jnp.where(causal, b_col - b_row + li_r, -jnp.inf)
        inter = b_col + m_prev1
        m_t = jnp.maximum(inter, jnp.max(d, axis=-1, keepdims=True))
        s = lax.dot_general(q, k.astype(BF16), _NT, preferred_element_type=F32)
        w = jnp.exp(d - m_t) * s
        e_inter = jnp.exp(inter - m_t)
        c_aug = c_ref[h]
        num = (jnp.dot(w.astype(BF16), vaug, preferred_element_type=F32)
               + e_inter * jnp.dot(q, c_aug.astype(BF16), preferred_element_type=F32))
        den = num[:, M_DV:M_DV + 1]
        hh = num[:, :M_DV] / jnp.maximum(jnp.abs(den), jnp.exp(-m_t))
        mu = jnp.mean(hh, axis=-1, keepdims=True)
        dh = hh - mu
        var = jnp.mean(dh * dh, axis=-1, keepdims=True)
        hn = dh * lax.rsqrt(var + LN_EPS)
        sl = slice(h * M_DV, (h + 1) * M_DV)
        out_ref[:, sl] = hn * ng_ref[:, sl] * _sigmoid(o_ref[:, sl])

        b_last = b_col[L - 1:L, :]
        g_col = b_last - b_col + li_col
        m_new = jnp.maximum(b_last + m_prev1, jnp.max(g_col, axis=0, keepdims=True))
        decay = jnp.exp(b_last + m_prev1 - m_new)
        wk = (jnp.exp(g_col - m_new) * k).astype(BF16)
        c_ref[h] = decay * c_aug + lax.dot_general(wk, vaug, _TN, preferred_element_type=F32)
        m_ref[h] = jnp.broadcast_to(m_new, (1, LANES))


def _mlstm_core(qk, v, o, gates, conv_w, conv_b, norm_g, bn, seq):
    L = CHUNK
    nc = seq // L
    tok = lambda w: pl.BlockSpec((bn, L, w), lambda c: (0, c, 0))
    full = lambda r, w: pl.BlockSpec((r, w), lambda c: (0, 0))
    per_batch = lambda a: a.reshape(bn, seq, a.shape[-1])
    out = pl.pallas_call(
        _mlstm_kernel,
        grid=(nc,),
        in_specs=[tok(N_QK), tok(N_V), tok(D_MODEL), tok(LANES),
                  full(M_CONV, N_QK), full(1, N_QK), full(1, N_V)],
        out_specs=tok(N_V),
        out_shape=jax.ShapeDtypeStruct((bn, seq, N_V), F32),
        scratch_shapes=[
            pltpu.VMEM((bn, L + SUBLANES, N_QK), F32),
            pltpu.VMEM((bn, M_HEADS, M_DQK, M_DV + LANES), F32),
            pltpu.VMEM((bn, M_HEADS, 1, LANES), F32),
        ],
        compiler_params=_cparams(1),
        name="mlstm",
    )(per_batch(qk), per_batch(v), per_batch(o), per_batch(gates),
      conv_w, conv_b.reshape(1, N_QK), norm_g.reshape(1, N_V))
    return out.reshape(bn * seq, N_V)


def _mlstm_layer(x, mod, w_in, conv_w, conv_b, b_if, norm_g, w_out, lng, lnb, bn, seq):
    sh, sc, gate = mod
    w_qk = w_in[:, :N_QK].astype(BF16)
    w_v = w_in[:, N_QK:N_QK + N_V].astype(BF16)
    w_o = w_in[:, N_QK + N_V:N_QK + N_V + D_MODEL].astype(BF16)
    w_g = jnp.zeros((D_MODEL, LANES), F32).at[:, :2 * M_HEADS].set(w_in[:, N_QK + N_V + D_MODEL:])
    b_g = jnp.zeros((1, LANES), F32).at[0, :2 * M_HEADS].set(b_if)
    z = lambda n: jnp.zeros((1, n), F32)
    qk, v, o, gates = _proj(x, sc, sh, [w_qk, w_v, w_o, w_g], [z(N_QK), z(N_V), z(D_MODEL), b_g],
                            [False, False, False, True], seq)
    hg = _mlstm_core(qk, v, o, gates, conv_w, conv_b, norm_g, bn, seq)
    return _lin_ln(hg, x, gate, w_out.astype(BF16), lng, lnb, seq)


def _s5_in_kernel(x_ref, sc_ref, sh_ref, wt_ref, o_ref):
    h = x_ref[...] * (1.0 + sc_ref[...]) + sh_ref[...]
    ut = lax.dot_general(wt_ref[...], h.astype(BF16), _NT, preferred_element_type=F32)
    for cl in range(o_ref.shape[1]):
        o_ref[:, cl, :, :] = ut[:, cl * LANES:(cl + 1) * LANES].reshape(D_MODEL // SUBLANES, SUBLANES, LANES)


def _s5_in(x, sc, sh, w_t, seq, tm=512):
    t, d = x.shape
    tpb = seq // tm
    nch = tm // LANES
    vec = pl.BlockSpec((None, 1, d), lambda i: (i // tpb, 0, 0))
    return pl.pallas_call(
        _s5_in_kernel,
        grid=(t // tm,),
        in_specs=[pl.BlockSpec((tm, d), lambda i: (i, 0)), vec, vec,
                  pl.BlockSpec((d, d), lambda i: (0, 0))],
        out_specs=pl.BlockSpec((d // SUBLANES, nch, SUBLANES, LANES), lambda i: (0, i, 0, 0)),
        out_shape=jax.ShapeDtypeStruct((d // SUBLANES, t // LANES, SUBLANES, LANES), F32),
        compiler_params=_cparams(1),
        name="s5_in",
    )(x, sc, sh, w_t)


def _cpow(lr, lim, dt, steps):
    mag = jnp.exp(lr * dt * steps)
    ang = lim * dt * steps
    return mag * jnp.cos(ang), mag * jnp.sin(ang)


def _s5_kernel(nc, u_ref, lamr_ref, lamc_ref, ldt_ref, crep_ref, cirep_ref, brt_ref, bit_ref,
               brtt_ref, bitt_ref, crt_ref, cit_ref, y_ref, acc_ref, kt_ref, r_ref, toe_ref, a_ref, xin_ref):
    L = CHUNK
    G = S5_GROUP
    P = S5_STATE
    nct = acc_ref.shape[1]

    def u_rows(i):
        return u_ref[i // SUBLANES, pl.ds(i % SUBLANES, nct, stride=SUBLANES), :]

    dt = jnp.exp(ldt_ref[...])
    lr_r, lim_r = lamr_ref[0:1, :], lamr_ref[1:2, :]
    lr_c, lim_c = lamc_ref[:, 0:1], lamc_ref[:, 1:2]

    a_re, a_im = _cpow(lr_r, lim_r, dt, 1.0)
    lam_sq = lr_r * lr_r + lim_r * lim_r
    t_re = ((a_re - 1.0) * lr_r + a_im * lim_r) / lam_sq
    t_im = (a_im * lr_r - (a_re - 1.0) * lim_r) / lam_sq

    bbt_re = t_re * brt_ref[...] - t_im * bit_ref[...]
    bbt_im = t_re * bit_ref[...] + t_im * brt_ref[...]
    bbtt_re = t_re * brtt_ref[...] - t_im * bitt_ref[...]
    bbtt_im = t_re * bitt_ref[...] + t_im * brtt_ref[...]
    cb_re = crep_ref[...] * bbtt_re - cirep_ref[...] * bbtt_im
    cb_im = crep_ref[...] * bbtt_im + cirep_ref[...] * bbtt_re

    lag = lax.broadcasted_iota(I32, (P, L), 1).astype(F32)
    p0_re, p0_im = _cpow(lr_c, lim_c, dt, lag)
    p1_re, p1_im = _cpow(lr_c, lim_c, dt, lag + 1.0)
    kt_ref[...] = _dot3(cb_re, p0_re) - _dot3(cb_im, p0_im)

    for o in range(G):
        cr_o = crt_ref[:, o:o + 1]
        ci_o = cit_ref[:, o:o + 1]
        cols = slice((o % 2) * L, (o % 2 + 1) * L)
        r_ref[o // 2, 0:P, cols] = (cr_o * p1_re - ci_o * p1_im).astype(BF16)
        r_ref[o // 2, P:2 * P, cols] = (-(cr_o * p1_im + ci_o * p1_re)).astype(BF16)

    for i in range(G):
        a_ref[:, i * L:(i + 1) * L] = u_rows(i).astype(BF16)

    back = (L - 1.0) - lax.broadcasted_iota(I32, (L, P), 0).astype(F32)
    ps_re, ps_im = _cpow(lr_r, lim_r, dt, back)
    for i in range(G):
        bi_re = bbt_re[i:i + 1, :]
        bi_im = bbt_im[i:i + 1, :]
        w_i = jnp.concatenate([ps_re * bi_re - ps_im * bi_im, ps_re * bi_im + ps_im * bi_re], axis=1)
        toe_ref[0, i * L:(i + 1) * L, 0:2 * P] = w_i.astype(BF16)
    e_acc = jnp.dot(a_ref[...], toe_ref[0, :, 0:2 * P], preferred_element_type=F32)

    chunk_in_seq = lax.broadcasted_iota(I32, (nct, 2 * P), 0) % nc
    x = e_acc
    dist = 1
    while dist < nc:
        d_re, d_im = _cpow(lr_r, lim_r, dt, float(L * dist))
        m1 = jnp.concatenate([d_re, d_re], axis=1)
        m2 = jnp.concatenate([-d_im, d_im], axis=1)
        shifted = jnp.where(chunk_in_seq >= dist, pltpu.roll(x, dist, 0), 0.0)
        x = x + shifted * m1 + pltpu.roll(shifted, P, 1) * m2
        dist *= 2
    x_in = jnp.where(chunk_in_seq >= 1, pltpu.roll(x, 1, 0), 0.0)

    xin_ref[...] = x_in.astype(BF16)

    srow = lax.broadcasted_iota(I32, (L, L), 0)
    tcol = lax.broadcasted_iota(I32, (L, L), 1)
    keep = tcol >= srow

    def toeplitz(o, i):
        k_row = jnp.broadcast_to(kt_ref[pl.ds(o * G + i, 1), :], (L, L))
        return jnp.where(keep, pltpu.roll(k_row, 0, 1, stride=1, stride_axis=0), 0.0).astype(BF16)

    def build(op, slot):
        for i in range(G):
            for oo in range(2):
                toe_ref[slot, i * L:(i + 1) * L, oo * L:(oo + 1) * L] = toeplitz(2 * op + oo, i)

    def emit(op, slot):
        acc_ref[op] = (jnp.dot(xin_ref[...], r_ref[op], preferred_element_type=F32)
                       + jnp.dot(a_ref[...], toe_ref[slot], preferred_element_type=F32))

    n_op = G // 2
    build(0, 0)

    def two_pairs(q, _):
        build(2 * q + 1, 1)
        emit(2 * q, 0)
        build(jnp.minimum(2 * q + 2, n_op - 1), 0)
        emit(2 * q + 1, 1)
        return 0

    lax.fori_loop(0, n_op // 2, two_pairs, 0)

    for o in range(G):
        y_ref[o // SUBLANES, pl.ds(o % SUBLANES, nct, stride=SUBLANES), :] = (
            acc_ref[o // 2, :, (o % 2) * L:(o % 2 + 1) * L])


def _s5_core(u4, lam_re, lam_im, log_dt, b_re, b_im, c_re, c_im, bn, seq):
    G, P, L = S5_GROUP, S5_STATE, CHUNK
    ng = S5_GROUPS
    nc = seq // L
    nct = bn * nc
    rt = G // SUBLANES
    u3 = u4.reshape(D_MODEL // SUBLANES, nct * SUBLANES, L)
    lam_r = jnp.stack([lam_re, lam_im], axis=1)
    lam_c = jnp.stack([lam_re, lam_im], axis=2)
    brt = jnp.swapaxes(b_re, 1, 2)
    bit = jnp.swapaxes(b_im, 1, 2)
    spec = lambda a: pl.BlockSpec((None,) + a.shape[1:], lambda g: (g,) + (0,) * (a.ndim - 1))
    params = [
        lam_r, lam_c, log_dt.reshape(ng, 1, 1),
        jnp.repeat(c_re, G, axis=1), jnp.repeat(c_im, G, axis=1),
        brt, bit,
        jnp.tile(brt, (1, G, 1)), jnp.tile(bit, (1, G, 1)),
        jnp.swapaxes(c_re, 1, 2), jnp.swapaxes(c_im, 1, 2),
    ]
    y3 = pl.pallas_call(
        functools.partial(_s5_kernel, nc),
        grid=(ng,),
        in_specs=[pl.BlockSpec((rt, nct * SUBLANES, L), lambda g: (g, 0, 0))] + [spec(a) for a in params],
        out_specs=pl.BlockSpec((rt, nct * SUBLANES, L), lambda g: (g, 0, 0)),
        out_shape=jax.ShapeDtypeStruct(u3.shape, F32),
        scratch_shapes=[
            pltpu.VMEM((G // 2, nct, 2 * L), F32),
            pltpu.VMEM((G * G, L), F32),
            pltpu.VMEM((G // 2, 2 * P, 2 * L), BF16),
            pltpu.VMEM((2, G * L, 2 * L), BF16),
            pltpu.VMEM((nct, G * L), BF16),
            pltpu.VMEM((nct, 2 * P), BF16),
        ],
        compiler_params=_cparams(1),
        name="s5_core",
    )(u3, *params)
    return u3, y3


def _s5_out_kernel(u_ref, y_ref, dsk_ref, x_ref, g_ref, w_ref, lng_ref, lnb_ref, o_ref):
    nch = u_ref.shape[1] // SUBLANES
    for cl in range(nch):
        rows = slice(cl * SUBLANES, (cl + 1) * SUBLANES)
        v = _gelu_tanh(y_ref[:, rows, :] + dsk_ref[...] * u_ref[:, rows, :])
        vt = v.reshape(D_MODEL, LANES).T
        z = jnp.dot(vt.astype(BF16), w_ref[...], preferred_element_type=F32)
        y = z[:, :D_MODEL] * _sigmoid(z[:, D_MODEL:])
        tok = slice(cl * LANES, (cl + 1) * LANES)
        o_ref[tok, :] = _layer_norm(DN_ALPHA * x_ref[tok, :] + g_ref[...] * y, lng_ref[...], lnb_ref[...])


def _s5_out(u3, y3, d_skip, x, gate, w_glu, lng, lnb, seq, tm=512):
    t, d = x.shape
    tpb = seq // tm
    nch = tm // LANES
    dsk = jnp.broadcast_to(d_skip.reshape(d // SUBLANES, SUBLANES, 1), (d // SUBLANES, SUBLANES, LANES))
    row = pl.BlockSpec((1, d), lambda i: (0, 0))
    tile3 = pl.BlockSpec((d // SUBLANES, nch * SUBLANES, LANES), lambda i: (0, i, 0))
    return pl.pallas_call(
        _s5_out_kernel,
        grid=(t // tm,),
        in_specs=[tile3, tile3,
                  pl.BlockSpec(dsk.shape, lambda i: (0, 0, 0)),
                  pl.BlockSpec((tm, d), lambda i: (i, 0)),
                  pl.BlockSpec((None, 1, d), lambda i: (i // tpb, 0, 0)),
                  pl.BlockSpec(w_glu.shape, lambda i: (0, 0)),
                  row, row],
        out_specs=pl.BlockSpec((tm, d), lambda i: (i, 0)),
        out_shape=jax.ShapeDtypeStruct((t, d), F32),
        compiler_params=_cparams(1),
        name="s5_out",
    )(u3, y3, dsk, x, gate, w_glu, lng.reshape(1, d), lnb.reshape(1, d))


def _s5_layer(x, mod, w_in, lam_re, lam_im, log_dt, b_re, b_im, c_re, c_im, d_skip, w_glu, lng, lnb, bn, seq):
    sh, sc, gate = mod
    u4 = _s5_in(x, sc, sh, w_in.T.astype(BF16), seq)
    u3, y3 = _s5_core(u4, lam_re, lam_im, log_dt, b_re, b_im, c_re, c_im, bn, seq)
    return _s5_out(u3, y3, d_skip, x, gate, w_glu.astype(BF16), lng, lnb, seq)


def _swa_kernel(sink_ref, q_ref, kvc_ref, kvp_ref, o_ref):
    L = WINDOW
    n = pl.program_id(1)
    qi = lax.broadcasted_iota(I32, (L, 2 * L), 0)
    kj = lax.broadcasted_iota(I32, (L, 2 * L), 1)
    kmin = jnp.where(n > 0, 0, L)
    valid = (kj > qi) & (kj <= qi + L) & (kj >= kmin)
    nkv = A_KV_HEADS * A_HEAD_DIM
    for h in range(A_KV_HEADS):
        ks = slice(h * A_HEAD_DIM, (h + 1) * A_HEAD_DIM)
        vs = slice(nkv + h * A_HEAD_DIM, nkv + (h + 1) * A_HEAD_DIM)
        kb = jnp.concatenate([kvp_ref[:, ks], kvc_ref[:, ks]], axis=0).astype(BF16)
        vb = jnp.concatenate([kvp_ref[:, vs], kvc_ref[:, vs]], axis=0).astype(BF16)
        for g in range(A_GROUP):
            hd = h * A_GROUP + g
            cs = slice(hd * A_HEAD_DIM, (hd + 1) * A_HEAD_DIM)
            q = (q_ref[:, cs] * (A_HEAD_DIM ** -0.5)).astype(BF16)
            s = lax.dot_general(q, kb, _NT, preferred_element_type=F32)
            s = jnp.where(valid, s, -jnp.inf)
            sink = sink_ref[hd]
            m = jnp.maximum(jnp.max(s, axis=-1, keepdims=True), sink)
            p = jnp.exp(s - m)
            den = jnp.sum(p, axis=-1, keepdims=True) + jnp.exp(sink - m)
            o_ref[:, cs] = jnp.dot(p.astype(BF16), vb, preferred_element_type=F32) / den


def _swa_core(qkv, sinks, bn, seq):
    L = WINDOW
    nb = seq // L
    nq = A_HEADS * A_HEAD_DIM
    nkv2 = 2 * A_KV_HEADS * A_HEAD_DIM
    kvcol = nq // nkv2
    return pl.pallas_call(
        _swa_kernel,
        grid_spec=pltpu.PrefetchScalarGridSpec(
            num_scalar_prefetch=1,
            grid=(bn, nb),
            in_specs=[
                pl.BlockSpec((L, nq), lambda b, n, s: (b * nb + n, 0)),
                pl.BlockSpec((L, nkv2), lambda b, n, s: (b * nb + n, kvcol)),
                pl.BlockSpec((L, nkv2), lambda b, n, s: (b * nb + jnp.maximum(n - 1, 0), kvcol)),
            ],
            out_specs=pl.BlockSpec((L, nq), lambda b, n, s: (b * nb + n, 0)),
        ),
        out_shape=jax.ShapeDtypeStruct((bn * seq, nq), F32),
        compiler_params=_cparams(2),
        name="swa",
    )(sinks, qkv, qkv, qkv)


def _swa_layer(x, mod, w_qkv, b_qkv, sinks, w_o, lng, lnb, bn, seq):
    sh, sc, gate = mod
    (qkv,) = _proj(x, sc, sh, [w_qkv.astype(BF16)], [b_qkv.reshape(1, -1)], [False], seq)
    o = _swa_core(qkv, sinks, bn, seq)
    return _lin_ln(o, x, gate, w_o.astype(BF16), lng, lnb, seq)


ROUTER_OFF = SUBLANES


def _router_kernel(x_ref, sc_ref, sh_ref, w_ref, b_ref, e_ref, rank_ref, wcol_ref, cnt_ref, carry_ref):
    tm = x_ref.shape[0]
    i = pl.program_id(0)

    @pl.when(i == 0)
    def _():
        carry_ref[...] = jnp.zeros_like(carry_ref)

    h = x_ref[...] * (1.0 + sc_ref[...]) + sh_ref[...]
    logits = _dot3(h, w_ref[...]) + b_ref[...]
    lt = logits.T
    gl = [lt[k:k + 1, :] for k in range(E_GROUPS)]
    gmax = jnp.maximum(jnp.maximum(gl[0], gl[1]), jnp.maximum(gl[2], gl[3]))
    g_sel = jnp.where(gl[0] == gmax, 0, jnp.where(gl[1] == gmax, 1, jnp.where(gl[2] == gmax, 2, 3)))
    p_g = 1.0 / (jnp.exp(gl[0] - gmax) + jnp.exp(gl[1] - gmax) + jnp.exp(gl[2] - gmax) + jnp.exp(gl[3] - gmax))
    el = [lt[ROUTER_OFF + k * E_PER_GROUP:ROUTER_OFF + (k + 1) * E_PER_GROUP, :] for k in range(E_GROUPS)]
    eg = jnp.where(g_sel == 0, el[0], jnp.where(g_sel == 1, el[1], jnp.where(g_sel == 2, el[2], el[3])))
    sub = lax.broadcasted_iota(I32, (E_PER_GROUP, tm), 0).astype(F32)
    v1 = jnp.max(eg, axis=0, keepdims=True)
    i1 = jnp.min(jnp.where(eg == v1, sub, float(E_PER_GROUP)), axis=0, keepdims=True)
    eg2 = jnp.where(sub == i1, -jnp.inf, eg)
    v2 = jnp.max(eg2, axis=0, keepdims=True)
    i2 = jnp.min(jnp.where(eg2 == v2, sub, float(E_PER_GROUP)), axis=0, keepdims=True)
    t2 = jnp.exp(v2 - v1)
    w1 = p_g / (1.0 + t2)
    w2 = w1 * t2
    e1 = g_sel * E_PER_GROUP + i1.astype(I32)
    e2 = g_sel * E_PER_GROUP + i2.astype(I32)

    sub8 = lax.broadcasted_iota(I32, (SUBLANES, tm), 0)
    e_ref[...] = jnp.where(sub8 == 0, e1, jnp.where(sub8 == 1, e2, 0))
    subl = lax.broadcasted_iota(I32, (LANES, tm), 0)
    wcol_ref[...] = jnp.where(subl == 0, w1, jnp.where(subl == 1, w2, 0.0)).T

    eid = lax.broadcasted_iota(I32, (N_EXPERTS, tm), 0)
    oh1 = eid == e1
    oh2 = eid == e2
    ohs = jnp.where(oh1 | oh2, 1.0, 0.0)
    upper = (lax.broadcasted_iota(I32, (tm, tm), 0) <= lax.broadcasted_iota(I32, (tm, tm), 1)).astype(BF16)
    incl = jnp.dot(ohs.astype(BF16), upper, preferred_element_type=F32)
    base = incl - ohs + carry_ref[:, 0:1]
    r1 = jnp.sum(jnp.where(oh1, base, 0.0), axis=0, keepdims=True)
    r2 = jnp.sum(jnp.where(oh2, base, 0.0), axis=0, keepdims=True)
    rank_ref[...] = jnp.where(sub8 == 0, r1, jnp.where(sub8 == 1, r2, 0.0)).astype(I32)
    carry_ref[...] = carry_ref[...] + jnp.sum(ohs, axis=1, keepdims=True)
    cnt_ref[...] = carry_ref[...]


def _router(x, sc, sh, w_group, b_group, w_router, b_router, seq, tm=512):
    t, d = x.shape
    tpb = seq // tm
    w = jnp.zeros((d, LANES), F32).at[:, :E_GROUPS].set(w_group).at[:, ROUTER_OFF:ROUTER_OFF + N_EXPERTS].set(w_router)
    b = jnp.zeros((1, LANES), F32).at[0, :E_GROUPS].set(b_group).at[0, ROUTER_OFF:ROUTER_OFF + N_EXPERTS].set(b_router)
    vec = pl.BlockSpec((None, 1, d), lambda i: (i // tpb, 0, 0))
    rows = pl.BlockSpec((SUBLANES, tm), lambda i: (0, i))
    return pl.pallas_call(
        _router_kernel,
        grid=(t // tm,),
        in_specs=[pl.BlockSpec((tm, d), lambda i: (i, 0)), vec, vec,
                  pl.BlockSpec((d, LANES), lambda i: (0, 0)), pl.BlockSpec((1, LANES), lambda i: (0, 0))],
        out_specs=[rows, rows, pl.BlockSpec((tm, LANES), lambda i: (i, 0)),
                   pl.BlockSpec((N_EXPERTS, LANES), lambda i: (0, 0))],
        out_shape=[jax.ShapeDtypeStruct((SUBLANES, t), I32), jax.ShapeDtypeStruct((SUBLANES, t), I32),
                   jax.ShapeDtypeStruct((t, LANES), F32), jax.ShapeDtypeStruct((N_EXPERTS, LANES), F32)],
        scratch_shapes=[pltpu.VMEM((N_EXPERTS, LANES), F32)],
        compiler_params=_cparams(1),
        name="router",
    )(x, sc, sh, w, b)


def _pos_kernel(e_ref, rank_ref, pstart_ref, pos_ref):
    tm = e_ref.shape[1]
    eid = lax.broadcasted_iota(I32, (N_EXPERTS, tm), 0)
    start = pstart_ref[:, 0:1]
    rows = []
    for k in range(TOP_K):
        base = jnp.sum(jnp.where(eid == e_ref[k:k + 1, :], start, 0.0), axis=0, keepdims=True)
        rows.append(base.astype(I32) + rank_ref[k:k + 1, :])
    sub8 = lax.broadcasted_iota(I32, (SUBLANES, tm), 0)
    pos_ref[...] = jnp.where(sub8 == 0, rows[0], jnp.where(sub8 == 1, rows[1], 0))


def _positions(e, rank, pstart, tm=2048):
    t = e.shape[1]
    tm = min(tm, t)
    rows = pl.BlockSpec((SUBLANES, tm), lambda i: (0, i))
    return pl.pallas_call(
        _pos_kernel,
        grid=(t // tm,),
        in_specs=[rows, rows, pl.BlockSpec((N_EXPERTS, LANES), lambda i: (0, 0))],
        out_specs=rows,
        out_shape=jax.ShapeDtypeStruct((SUBLANES, t), I32),
        compiler_params=_cparams(1),
        name="moe_pos",
    )(e, rank, jnp.broadcast_to(pstart.astype(F32)[:, None], (N_EXPERTS, LANES)))


SLAB = D_MODEL // LANES


def _to_slabs(ref, val):
    rows = val.shape[0]
    for cidx in range(SLAB):
        ref[pl.ds(cidx, rows, stride=SLAB), :] = val[:, cidx * LANES:(cidx + 1) * LANES]


def _from_slabs(ref, rows):
    return jnp.concatenate([ref[pl.ds(cidx, rows, stride=SLAB), :] for cidx in range(SLAB)], axis=1)


def _row_copy(src, src_row, dst, dst_row, sem):
    s0 = pl.multiple_of(src_row * SLAB, SLAB)
    d0 = pl.multiple_of(dst_row * SLAB, SLAB)
    return pltpu.make_async_copy(src.at[pl.ds(s0, SLAB), :], dst.at[pl.ds(d0, SLAB), :], sem)


def _dispatch_kernel(pend_ref, x_ref, sc_ref, sh_ref, pos_ref, buf_ref, hbuf, zbuf, sem, zsem):
    tm = x_ref.shape[0]
    i = pl.program_id(0)

    nt = pl.num_programs(0)
    slot = i % 2

    def zero_block(j):
        start = pl.multiple_of(j * (MOE_BM * SLAB), MOE_BM * SLAB)
        return pltpu.make_async_copy(zbuf, buf_ref.at[pl.ds(start, MOE_BM * SLAB), :], zsem)

    @pl.when(i == 0)
    def _():
        zbuf[...] = jnp.zeros_like(zbuf)

        def pad_block(e, go):
            end = pend_ref[e] // MOE_BM
            prev = jnp.where(e > 0, pend_ref[jnp.maximum(e - 1, 0)] // MOE_BM, 0)

            @pl.when(end > prev)
            def _():
                cp = zero_block(end - 1)
                cp.start() if go else cp.wait()
            return 0

        lax.fori_loop(0, N_EXPERTS, lambda e, c: pad_block(e, True), 0)
        lax.fori_loop(0, N_EXPERTS, lambda e, c: pad_block(e, False), 0)
        first_unused = pend_ref[N_EXPERTS - 1] // MOE_BM
        n_blocks = buf_ref.shape[0] // (MOE_BM * SLAB)
        lax.fori_loop(first_unused, n_blocks, lambda j, c: (zero_block(j).start(), c)[1], 0)
        lax.fori_loop(first_unused, n_blocks, lambda j, c: (zero_block(j).wait(), c)[1], 0)

    _to_slabs(hbuf.at[slot], x_ref[...] * (1.0 + sc_ref[...]) + sh_ref[...])

    def issue(rb, _):
        for u in range(SUBLANES):
            r = rb * SUBLANES + u
            for k in range(TOP_K):
                _row_copy(hbuf.at[slot], r, buf_ref, pos_ref[k, r], sem.at[slot]).start(priority=k)
        return 0

    lax.fori_loop(0, tm // SUBLANES, issue, 0)

    def drain(s):
        for k in range(TOP_K):
            pltpu.make_async_copy(hbuf.at[s], buf_ref.at[pl.ds(0, tm * SLAB), :], sem.at[s]).wait()

    @pl.when(i > 0)
    def _():
        drain(1 - slot)

    @pl.when(i == nt - 1)
    def _():
        drain(slot)


def _dispatch(x, sc, sh, pos, pend, n_rows, seq, tm=256):
    t, d = x.shape
    tpb = seq // tm
    nt = t // tm
    pos3 = pos[:TOP_K].reshape(TOP_K, nt, tm).transpose(1, 0, 2)
    vec = pl.BlockSpec((None, 1, d), lambda i, p: (i // tpb, 0, 0))
    return pl.pallas_call(
        _dispatch_kernel,
        grid_spec=pltpu.PrefetchScalarGridSpec(
            num_scalar_prefetch=1,
            grid=(nt,),
            in_specs=[pl.BlockSpec((tm, d), lambda i, p: (i, 0)), vec, vec,
                      pl.BlockSpec((None, TOP_K, tm), lambda i, p: (i, 0, 0), memory_space=pltpu.SMEM)],
            out_specs=pl.BlockSpec(memory_space=pl.ANY),
            scratch_shapes=[pltpu.VMEM((2, tm * SLAB, LANES), F32), pltpu.VMEM((MOE_BM * SLAB, LANES), F32),
                            pltpu.SemaphoreType.DMA((2,)), pltpu.SemaphoreType.DMA(())],
        ),
        out_shape=jax.ShapeDtypeStruct((n_rows * SLAB, LANES), F32),
        compiler_params=_cparams(1),
        name="moe_dispatch",
    )(pend, x, sc, sh, pos3)


def _experts_kernel(blk_e_ref, nb_ref, x_ref, w1_ref, w3_ref, w2_ref, y_ref, wb1, wb3, wb2):
    j = pl.program_id(0)

    @pl.when((j == 0) | (blk_e_ref[j] != blk_e_ref[jnp.maximum(j - 1, 0)]))
    def _():
        wb1[...] = w1_ref[...].astype(BF16)
        wb3[...] = w3_ref[...].astype(BF16)
        wb2[...] = w2_ref[...].astype(BF16)

    @pl.when(j < nb_ref[0])
    def _():
        xb = _from_slabs(x_ref, MOE_BM).astype(BF16)
        h1 = jnp.dot(xb, wb1[...], preferred_element_type=F32)
        h3 = jnp.dot(xb, wb3[...], preferred_element_type=F32)
        a = (_silu(h1) * h3).astype(BF16)
        _to_slabs(y_ref, jnp.dot(a, wb2[...], preferred_element_type=F32))

    @pl.when(j >= nb_ref[0])
    def _():
        y_ref[...] = jnp.zeros_like(y_ref)


def _experts(buf, blk_e, nb_used, w1, w3, w2, layer):
    nblk = buf.shape[0] // (MOE_BM * SLAB)
    d, de = w1.shape[2], w1.shape[3]
    row_blk = lambda j, be, nb: (jnp.minimum(j, nb[0] - 1), 0)
    wsel = lambda j, be, nb: (layer, be[j], 0, 0)
    return pl.pallas_call(
        _experts_kernel,
        grid_spec=pltpu.PrefetchScalarGridSpec(
            num_scalar_prefetch=2,
            grid=(nblk,),
            in_specs=[pl.BlockSpec((MOE_BM * SLAB, LANES), row_blk),
                      pl.BlockSpec((None, None, d, de), wsel), pl.BlockSpec((None, None, d, de), wsel),
                      pl.BlockSpec((None, None, de, d), wsel)],
            out_specs=pl.BlockSpec((MOE_BM * SLAB, LANES), lambda j, be, nb: (j, 0)),
            scratch_shapes=[pltpu.VMEM((d, de), BF16), pltpu.VMEM((d, de), BF16), pltpu.VMEM((de, d), BF16)],
        ),
        out_shape=jax.ShapeDtypeStruct(buf.shape, F32),
        compiler_params=_cparams(1),
        name="moe_experts",
    )(blk_e, nb_used, buf, w1, w3, w2)


def _combine_kernel(pos_ref, posn_ref, y_ref, wcol_ref, x_ref, g_ref, lng_ref, lnb_ref, o_ref, ybuf, sem):
    tm = x_ref.shape[0]
    i = pl.program_id(0)
    nt = pl.num_programs(0)
    slot = i % 2

    def gather(p_ref, s):
        def issue(rb, _):
            for u in range(SUBLANES):
                r = rb * SUBLANES + u
                for k in range(TOP_K):
                    _row_copy(y_ref, p_ref[k, r], ybuf.at[s, k], r, sem.at[s]).start(priority=k)
            return 0

        lax.fori_loop(0, tm // SUBLANES, issue, 0)

    @pl.when(i == 0)
    def _():
        gather(pos_ref, 0)

    @pl.when(i + 1 < nt)
    def _():
        gather(posn_ref, 1 - slot)

    for k in range(TOP_K):
        pltpu.make_async_copy(y_ref.at[pl.ds(0, tm * SLAB), :], ybuf.at[slot, k], sem.at[slot]).wait()
    y = (wcol_ref[:, 0:1] * _from_slabs(ybuf.at[slot, 0], tm)
         + wcol_ref[:, 1:2] * _from_slabs(ybuf.at[slot, 1], tm))
    o_ref[...] = _layer_norm(DN_ALPHA * x_ref[...] + g_ref[...] * y, lng_ref[...], lnb_ref[...])


def _combine(y, pos, wcol, x, gate, lng, lnb, seq, tm=256):
    t, d = x.shape
    tpb = seq // tm
    nt = t // tm
    pos3 = pos[:TOP_K].reshape(TOP_K, nt, tm).transpose(1, 0, 2)
    row = pl.BlockSpec((1, d), lambda i: (0, 0))
    return pl.pallas_call(
        _combine_kernel,
        grid=(nt,),
        in_specs=[pl.BlockSpec((None, TOP_K, tm), lambda i: (i, 0, 0), memory_space=pltpu.SMEM),
                  pl.BlockSpec((None, TOP_K, tm), lambda i: (jnp.minimum(i + 1, nt - 1), 0, 0),
                               memory_space=pltpu.SMEM),
                  pl.BlockSpec(memory_space=pl.ANY),
                  pl.BlockSpec((tm, LANES), lambda i: (i, 0)),
                  pl.BlockSpec((tm, d), lambda i: (i, 0)),
                  pl.BlockSpec((None, 1, d), lambda i: (i // tpb, 0, 0)),
                  row, row],
        out_specs=pl.BlockSpec((tm, d), lambda i: (i, 0)),
        out_shape=jax.ShapeDtypeStruct((t, d), F32),
        scratch_shapes=[pltpu.VMEM((2, TOP_K, tm * SLAB, LANES), F32), pltpu.SemaphoreType.DMA((2,))],
        compiler_params=_cparams(1),
        name="moe_combine",
    )(pos3, pos3, y, wcol, x, gate, lng.reshape(1, d), lnb.reshape(1, d))


def _moe_layer(x, mod, w_group, b_group, w_router, b_router, w1, w3, w2, layer, lng, lnb, seq):
    sh, sc, gate = mod
    t, d = x.shape
    e, rank, wcol, cnt = _router(x, sc, sh, w_group, b_group, w_router, b_router, seq)
    counts = cnt[:, 0].astype(I32)
    padded = (counts + MOE_BM - 1) // MOE_BM * MOE_BM
    pend = jnp.cumsum(padded)
    pstart = pend - padded
    n_rows = t * TOP_K + N_EXPERTS * MOE_BM
    nblk = n_rows // MOE_BM
    blk_start = jnp.arange(nblk, dtype=I32) * MOE_BM
    blk_e = jnp.minimum(jnp.sum(pend[None, :] <= blk_start[:, None], axis=1), N_EXPERTS - 1).astype(I32)
    nb_used = (pend[-1:] // MOE_BM).astype(I32)
    pos = _positions(e, rank, pstart)
    buf = _dispatch(x, sc, sh, pos, pend.astype(I32), n_rows, seq)
    y = _experts(buf, blk_e, nb_used, w1, w3, w2, layer)
    return _combine(y, pos, wcol, x, gate, lng, lnb, seq)


def kernel(x, c, ada_w, ada_b, ln1_g, ln1_b, ln2_g, ln2_b, mlstm_w_in, mlstm_conv_w, mlstm_conv_b, mlstm_b_if, mlstm_norm_g, mlstm_w_out, s5_w_in, s5_lam_re, s5_lam_im, s5_log_dt, s5_b_re, s5_b_im, s5_c_re, s5_c_im, s5_d, s5_w_glu, swa_w_qkv, swa_b_qkv, swa_sinks, swa_w_o, moe_w_group, moe_b_group, moe_w_router, moe_b_router, moe_w1, moe_w3, moe_w2):
    bn, seq, d = x.shape
    depth = ada_w.shape[0]
    mods = _ada(c, ada_w, ada_b).reshape(depth, bn, 6, 1, d)
    xt = x.reshape(bn * seq, d)
    for i in range(depth):
        m1 = tuple(mods[i, :, k] for k in range(3))
        m2 = tuple(mods[i, :, k] for k in range(3, 6))
        kind, j = i % N_MIXERS, i // N_MIXERS
        if kind == 0:
            xt = _mlstm_layer(xt, m1, mlstm_w_in[j], mlstm_conv_w[j], mlstm_conv_b[j], mlstm_b_if[j],
                              mlstm_norm_g[j], mlstm_w_out[j], ln1_g[i], ln1_b[i], bn, seq)
        elif kind == 1:
            xt = _s5_layer(xt, m1, s5_w_in[j], s5_lam_re[j], s5_lam_im[j], s5_log_dt[j], s5_b_re[j], s5_b_im[j],
                           s5_c_re[j], s5_c_im[j], s5_d[j], s5_w_glu[j], ln1_g[i], ln1_b[i], bn, seq)
        else:
            xt = _swa_layer(xt, m1, swa_w_qkv[j], swa_b_qkv[j], swa_sinks[j], swa_w_o[j],
                            ln1_g[i], ln1_b[i], bn, seq)
        xt = _moe_layer(xt, m2, moe_w_group[i], moe_b_group[i], moe_w_router[i], moe_b_router[i],
                        moe_w1, moe_w3, moe_w2, i, ln2_g[i], ln2_b[i], seq)
    return xt.reshape(bn, seq, d)
```

```python
import functools
import math

import jax
import jax.numpy as jnp
from jax import lax
from jax.experimental import pallas as pl
from jax.experimental.pallas import tpu as pltpu

F32 = jnp.float32
BF16 = jnp.bfloat16
I32 = jnp.int32

D_MODEL = 1024
DEPTH = 4
N_MIXERS = 3
DN_ALPHA = (2 * DEPTH) ** 0.25
LN_EPS = 1e-5

M_HEADS = 4
M_DQK = D_MODEL // 8
M_DV = D_MODEL // M_HEADS
M_CONV = 4
N_QK = 2 * M_HEADS * M_DQK
N_V = M_HEADS * M_DV

S5_GROUP = 16
S5_GROUPS = D_MODEL // S5_GROUP
S5_STATE = 64

A_HEADS = 16
A_KV_HEADS = 4
A_GROUP = A_HEADS // A_KV_HEADS
A_HEAD_DIM = D_MODEL // A_HEADS
WINDOW = 128

E_GROUPS = 4
E_PER_GROUP = 8
N_EXPERTS = E_GROUPS * E_PER_GROUP
TOP_K = 2
D_EXPERT = 512

LANES = 128
SUBLANES = 8
CHUNK = LANES
MOE_BM = 512
VMEM_LIMIT = 56 * 1024 * 1024


def _cparams(n_axes):
    return pltpu.CompilerParams(dimension_semantics=("arbitrary",) * n_axes, vmem_limit_bytes=VMEM_LIMIT)


def _split_bf16(a):
    hi = a.astype(BF16)
    lo = (a - hi.astype(F32)).astype(BF16)
    return hi, lo


def _dot3(a, b, dims=None):
    ah, al = _split_bf16(a)
    bh, bl = _split_bf16(b)
    if dims is None:
        dot = lambda x, y: jnp.dot(x, y, preferred_element_type=F32)
    else:
        dot = lambda x, y: lax.dot_general(x, y, dims, preferred_element_type=F32)
    return dot(ah, bh) + (dot(ah, bl) + dot(al, bh))


def _bdot(a, b):
    return jnp.dot(a.astype(BF16), b.astype(BF16), preferred_element_type=F32)


_NT = (((1,), (1,)), ((), ()))
_TN = (((0,), (0,)), ((), ()))


def _layer_norm(v, g, b):
    mu = jnp.mean(v, axis=-1, keepdims=True)
    d = v - mu
    var = jnp.mean(d * d, axis=-1, keepdims=True)
    return d * lax.rsqrt(var + LN_EPS) * g + b


def _sigmoid(x):
    return 1.0 / (1.0 + jnp.exp(-x))


def _silu(x):
    return x * _sigmoid(x)


def _log_sigmoid(x):
    return jnp.minimum(x, 0.0) - jnp.log1p(jnp.exp(-jnp.abs(x)))


def _gelu_tanh(x):
    return 0.5 * x * (1.0 + jnp.tanh(math.sqrt(2.0 / math.pi) * (x + 0.044715 * (x * x * x))))


def _ada_kernel(c_ref, w_ref, b_ref, o_ref):
    cond = _silu(c_ref[...])
    o_ref[...] = _dot3(cond, w_ref[...]) + b_ref[...]


def _ada(c, ada_w, ada_b):
    bn, d = c.shape
    depth, _, n = ada_w.shape
    tn = 1536
    c8 = jnp.zeros((SUBLANES, d), F32).at[:bn].set(c)
    out = pl.pallas_call(
        _ada_kernel,
        grid=(depth, n // tn),
        in_specs=[
            pl.BlockSpec((SUBLANES, d), lambda l, j: (0, 0)),
            pl.BlockSpec((None, d, tn), lambda l, j: (l, 0, j)),
            pl.BlockSpec((None, 1, tn), lambda l, j: (l, 0, j)),
        ],
        out_specs=pl.BlockSpec((None, SUBLANES, tn), lambda l, j: (l, 0, j)),
        out_shape=jax.ShapeDtypeStruct((depth, SUBLANES, n), F32),
        compiler_params=_cparams(2),
        name="ada",
    )(c8, ada_w, ada_b.reshape(depth, 1, n))
    return out[:, :bn]


def _proj_kernel(precise, x_ref, sc_ref, sh_ref, *refs):
    n = len(precise)
    w_refs, b_refs, o_refs = refs[:n], refs[n:2 * n], refs[2 * n:]
    h = x_ref[...] * (1.0 + sc_ref[...]) + sh_ref[...]
    hb = h.astype(BF16)
    for w_ref, b_ref, o_ref, p in zip(w_refs, b_refs, o_refs, precise):
        if p:
            r = _dot3(h, w_ref[...])
        else:
            r = jnp.dot(hb, w_ref[...], preferred_element_type=F32)
        o_ref[...] = (r + b_ref[...]).astype(o_ref.dtype)


def _proj(x, sc, sh, ws, bs, precise, seq, out_dtypes=None, tm=512):
    out_dtypes = out_dtypes or [F32] * len(ws)
    t, d = x.shape
    tpb = seq // tm
    vec = pl.BlockSpec((None, 1, d), lambda i: (i // tpb, 0, 0))
    in_specs = [pl.BlockSpec((tm, d), lambda i: (i, 0)), vec, vec]
    in_specs += [pl.BlockSpec(w.shape, lambda i: (0, 0)) for w in ws]
    in_specs += [pl.BlockSpec(b.shape, lambda i: (0, 0)) for b in bs]
    return pl.pallas_call(
        functools.partial(_proj_kernel, tuple(precise)),
        grid=(t // tm,),
        in_specs=in_specs,
        out_specs=[pl.BlockSpec((tm, w.shape[1]), lambda i: (i, 0)) for w in ws],
        out_shape=[jax.ShapeDtypeStruct((t, w.shape[1]), dt) for w, dt in zip(ws, out_dtypes)],
        compiler_params=_cparams(1),
        name="proj",
    )(x, sc, sh, *ws, *bs)


def _lin_ln_kernel(a_ref, x_ref, g_ref, w_ref, lng_ref, lnb_ref, o_ref):
    y = jnp.dot(a_ref[...].astype(BF16), w_ref[...], preferred_element_type=F32)
    o_ref[...] = _layer_norm(DN_ALPHA * x_ref[...] + g_ref[...] * y, lng_ref[...], lnb_ref[...])


def _lin_ln(a, x, gate, w, lng, lnb, seq, tm=512):
    t, d = x.shape
    k = a.shape[1]
    tpb = seq // tm
    row = pl.BlockSpec((1, d), lambda i: (0, 0))
    return pl.pallas_call(
        _lin_ln_kernel,
        grid=(t // tm,),
        in_specs=[
            pl.BlockSpec((tm, k), lambda i: (i, 0)),
            pl.BlockSpec((tm, d), lambda i: (i, 0)),
            pl.BlockSpec((None, 1, d), lambda i: (i // tpb, 0, 0)),
            pl.BlockSpec((k, d), lambda i: (0, 0)),
            row, row,
        ],
        out_specs=pl.BlockSpec((tm, d), lambda i: (i, 0)),
        out_shape=jax.ShapeDtypeStruct((t, d), F32),
        compiler_params=_cparams(1),
        name="lin_ln",
    )(a, x, gate, w, lng.reshape(1, d), lnb.reshape(1, d))


def _mlstm_kernel(qk_ref, v_ref, o_ref, gates_ref, cw_ref, cb_ref, ng_ref, out_ref,
                  qkbuf, c_ref, m_ref):
    @pl.when(pl.program_id(0) == 0)
    def _():
        qkbuf[...] = jnp.zeros_like(qkbuf)
        c_ref[...] = jnp.zeros_like(c_ref)
        m_ref[...] = jnp.zeros_like(m_ref)

    for b in range(qk_ref.shape[0]):
        _mlstm_chunk(qk_ref.at[b], v_ref.at[b], o_ref.at[b], gates_ref.at[b], cw_ref, cb_ref, ng_ref,
                     out_ref.at[b], qkbuf.at[b], c_ref.at[b], m_ref.at[b])


def _mlstm_chunk(qk_ref, v_ref, o_ref, gates_ref, cw_ref, cb_ref, ng_ref, out_ref, qkbuf, c_ref, m_ref):
    L = CHUNK
    qkbuf[SUBLANES:SUBLANES + L, :] = qk_ref[...]
    conv = cb_ref[...] + cw_ref[0:1, :] * qkbuf[pl.ds(SUBLANES - 3, L), :]
    for j in range(1, M_CONV):
        conv = conv + cw_ref[j:j + 1, :] * qkbuf[pl.ds(SUBLANES - 3 + j, L), :]
    qk = _silu(conv)
    qkbuf[0:SUBLANES, :] = qkbuf[L:L + SUBLANES, :]

    gates = gates_ref[...]
    gates_t = gates.T
    lf_col = _log_sigmoid(gates)
    lf_row = _log_sigmoid(gates_t[0:2 * SUBLANES, :])
    li_row = gates_t[0:M_HEADS, :]
    row_i = lax.broadcasted_iota(I32, (L, L), 0)
    col_i = lax.broadcasted_iota(I32, (L, L), 1)
    causal = col_i <= row_i
    lower = causal.astype(BF16)
    upper = (row_i <= col_i).astype(BF16)
    lf_ch, lf_cl = _split_bf16(lf_col)
    b_col_all = (jnp.dot(lower, lf_ch, preferred_element_type=F32)
                 + jnp.dot(lower, lf_cl, preferred_element_type=F32))
    lf_rh, lf_rl = _split_bf16(lf_row)
    b_row_all = (jnp.dot(lf_rh, upper, preferred_element_type=F32)
                 + jnp.dot(lf_rl, upper, preferred_element_type=F32))
    ones_col = (lax.broadcasted_iota(I32, (L, LANES), 1) == 0).astype(BF16)

    for h in range(M_HEADS):
        q = qk[:, h * M_DQK:(h + 1) * M_DQK].astype(BF16)
        k = qk[:, N_QK // 2 + h * M_DQK:N_QK // 2 + (h + 1) * M_DQK] * (M_DQK ** -0.5)
        vaug = jnp.concatenate([v_ref[:, h * M_DV:(h + 1) * M_DV], ones_col], axis=1)
        b_col = b_col_all[:, M_HEADS + h:M_HEADS + h + 1]
        li_col = gates[:, h:h + 1]
        b_row = b_row_all[M_HEADS + h:M_HEADS + h + 1, :]
        li_r = li_row[h:h + 1, :]
        m_prev = m_ref[h]
        m_prev1 = m_prev[:, 0:1]
        d = jnp.where(causal, b_col - b_row + li_r, -jnp.inf)
        inter = b_col + m_prev1
        m_t = jnp.maximum(inter, jnp.max(d, axis=-1, keepdims=True))
        s = lax.dot_general(q, k.astype(BF16), _NT, preferred_element_type=F32)
        w = jnp.exp(d - m_t) * s
        e_inter = jnp.exp(inter - m_t)
        c_aug = c_ref[h]
        num = (jnp.dot(w.astype(BF16), vaug, preferred_element_type=F32)
               + e_inter * jnp.dot(q, c_aug.astype(BF16), preferred_element_type=F32))
        den = num[:, M_DV:M_DV + 1]
        hh = num[:, :M_DV] / jnp.maximum(jnp.abs(den), jnp.exp(-m_t))
        mu = jnp.mean(hh, axis=-1, keepdims=True)
        dh = hh - mu
        var = jnp.mean(dh * dh, axis=-1, keepdims=True)
        hn = dh * lax.rsqrt(var + LN_EPS)
        sl = slice(h * M_DV, (h + 1) * M_DV)
        out_ref[:, sl] = (hn * ng_ref[:, sl] * _sigmoid(o_ref[:, sl].astype(F32))).astype(out_ref.dtype)

        b_last = b_col[L - 1:L, :]
        g_col = b_last - b_col + li_col
        m_new = jnp.maximum(b_last + m_prev1, jnp.max(g_col, axis=0, keepdims=True))
        decay = jnp.exp(b_last + m_prev1 - m_new)
        wk = (jnp.exp(g_col - m_new) * k).astype(BF16)
        c_ref[h] = decay * c_aug + lax.dot_general(wk, vaug, _TN, preferred_element_type=F32)
        m_ref[h] = jnp.broadcast_to(m_new, (1, LANES))


def _mlstm_core(qk, v, o, gates, conv_w, conv_b, norm_g, bn, seq):
    L = CHUNK
    nc = seq // L
    tok = lambda w: pl.BlockSpec((bn, L, w), lambda c: (0, c, 0))
    full = lambda r, w: pl.BlockSpec((r, w), lambda c: (0, 0))
    per_batch = lambda a: a.reshape(bn, seq, a.shape[-1])
    out = pl.pallas_call(
        _mlstm_kernel,
        grid=(nc,),
        in_specs=[tok(N_QK), tok(N_V), tok(D_MODEL), tok(LANES),
                  full(M_CONV, N_QK), full(1, N_QK), full(1, N_V)],
        out_specs=tok(N_V),
        out_shape=jax.ShapeDtypeStruct((bn, seq, N_V), BF16),
        scratch_shapes=[
            pltpu.VMEM((bn, L + SUBLANES, N_QK), F32),
            pltpu.VMEM((bn, M_HEADS, M_DQK, M_DV + LANES), F32),
            pltpu.VMEM((bn, M_HEADS, 1, LANES), F32),
        ],
        compiler_params=_cparams(1),
        name="mlstm",
    )(per_batch(qk), per_batch(v), per_batch(o), per_batch(gates),
      conv_w, conv_b.reshape(1, N_QK), norm_g.reshape(1, N_V))
    return out.reshape(bn * seq, N_V)


def _mlstm_layer(x, mod, w_in, conv_w, conv_b, b_if, norm_g, w_out, lng, lnb, bn, seq):
    sh, sc, gate = mod
    w_qk = w_in[:, :N_QK].astype(BF16)
    w_v = w_in[:, N_QK:N_QK + N_V].astype(BF16)
    w_o = w_in[:, N_QK + N_V:N_QK + N_V + D_MODEL].astype(BF16)
    w_g = jnp.zeros((D_MODEL, LANES), F32).at[:, :2 * M_HEADS].set(w_in[:, N_QK + N_V + D_MODEL:])
    b_g = jnp.zeros((1, LANES), F32).at[0, :2 * M_HEADS].set(b_if)
    z = lambda n: jnp.zeros((1, n), F32)
    qk, v, o, gates = _proj(x, sc, sh, [w_qk, w_v, w_o, w_g], [z(N_QK), z(N_V), z(D_MODEL), b_g],
                            [False, False, False, True], seq, out_dtypes=[F32, BF16, BF16, F32])
    hg = _mlstm_core(qk, v, o, gates, conv_w, conv_b, norm_g, bn, seq)
    return _lin_ln(hg, x, gate, w_out.astype(BF16), lng, lnb, seq)


def _s5_in_kernel(x_ref, sc_ref, sh_ref, wt_ref, o_ref):
    h = x_ref[...] * (1.0 + sc_ref[...]) + sh_ref[...]
    ut = lax.dot_general(wt_ref[...], h.astype(BF16), _NT, preferred_element_type=F32)
    for cl in range(o_ref.shape[1]):
        o_ref[:, cl, :, :] = ut[:, cl * LANES:(cl + 1) * LANES].reshape(D_MODEL // SUBLANES, SUBLANES, LANES)


def _s5_in(x, sc, sh, w_t, seq, tm=512):
    t, d = x.shape
    tpb = seq // tm
    nch = tm // LANES
    vec = pl.BlockSpec((None, 1, d), lambda i: (i // tpb, 0, 0))
    return pl.pallas_call(
        _s5_in_kernel,
        grid=(t // tm,),
        in_specs=[pl.BlockSpec((tm, d), lambda i: (i, 0)), vec, vec,
                  pl.BlockSpec((d, d), lambda i: (0, 0))],
        out_specs=pl.BlockSpec((d // SUBLANES, nch, SUBLANES, LANES), lambda i: (0, i, 0, 0)),
        out_shape=jax.ShapeDtypeStruct((d // SUBLANES, t // LANES, SUBLANES, LANES), F32),
        compiler_params=_cparams(1),
        name="s5_in",
    )(x, sc, sh, w_t)


def _cpow(lr, lim, dt, steps):
    mag = jnp.exp(lr * dt * steps)
    ang = lim * dt * steps
    return mag * jnp.cos(ang), mag * jnp.sin(ang)


def _s5_kernel(nc, u_ref, lamr_ref, lamc_ref, ldt_ref, crep_ref, cirep_ref, brt_ref, bit_ref,
               brtt_ref, bitt_ref, crt_ref, cit_ref, y_ref, acc_ref, kt_ref, r_ref, toe_ref, a_ref, xin_ref):
    L = CHUNK
    G = S5_GROUP
    P = S5_STATE
    nct = acc_ref.shape[1]

    def u_rows(i):
        return u_ref[i // SUBLANES, pl.ds(i % SUBLANES, nct, stride=SUBLANES), :]

    dt = jnp.exp(ldt_ref[...])
    lr_r, lim_r = lamr_ref[0:1, :], lamr_ref[1:2, :]
    lr_c, lim_c = lamc_ref[:, 0:1], lamc_ref[:, 1:2]

    a_re, a_im = _cpow(lr_r, lim_r, dt, 1.0)
    lam_sq = lr_r * lr_r + lim_r * lim_r
    t_re = ((a_re - 1.0) * lr_r + a_im * lim_r) / lam_sq
    t_im = (a_im * lr_r - (a_re - 1.0) * lim_r) / lam_sq

    bbt_re = t_re * brt_ref[...] - t_im * bit_ref[...]
    bbt_im = t_re * bit_ref[...] + t_im * brt_ref[...]
    bbtt_re = t_re * brtt_ref[...] - t_im * bitt_ref[...]
    bbtt_im = t_re * bitt_ref[...] + t_im * brtt_ref[...]
    cb_re = crep_ref[...] * bbtt_re - cirep_ref[...] * bbtt_im
    cb_im = crep_ref[...] * bbtt_im + cirep_ref[...] * bbtt_re

    lag = lax.broadcasted_iota(I32, (P, L), 1).astype(F32)
    p0_re, p0_im = _cpow(lr_c, lim_c, dt, lag)
    p1_re, p1_im = _cpow(lr_c, lim_c, dt, lag + 1.0)
    kt_ref[...] = _dot3(cb_re, p0_re) - _dot3(cb_im, p0_im)

    for o in range(G):
        cr_o = crt_ref[:, o:o + 1]
        ci_o = cit_ref[:, o:o + 1]
        cols = slice((o % 2) * L, (o % 2 + 1) * L)
        r_ref[o // 2, 0:P, cols] = (cr_o * p1_re - ci_o * p1_im).astype(BF16)
        r_ref[o // 2, P:2 * P, cols] = (-(cr_o * p1_im + ci_o * p1_re)).astype(BF16)

    for i in range(G):
        a_ref[:, i * L:(i + 1) * L] = u_rows(i).astype(BF16)

    back = (L - 1.0) - lax.broadcasted_iota(I32, (L, P), 0).astype(F32)
    ps_re, ps_im = _cpow(lr_r, lim_r, dt, back)
    for i in range(G):
        bi_re = bbt_re[i:i + 1, :]
        bi_im = bbt_im[i:i + 1, :]
        w_i = jnp.concatenate([ps_re * bi_re - ps_im * bi_im, ps_re * bi_im + ps_im * bi_re], axis=1)
        toe_ref[0, i * L:(i + 1) * L, 0:2 * P] = w_i.astype(BF16)
    e_acc = jnp.dot(a_ref[...], toe_ref[0, :, 0:2 * P], preferred_element_type=F32)

    chunk_in_seq = lax.broadcasted_iota(I32, (nct, 2 * P), 0) % nc
    x = e_acc
    dist = 1
    while dist < nc:
        d_re, d_im = _cpow(lr_r, lim_r, dt, float(L * dist))
        m1 = jnp.concatenate([d_re, d_re], axis=1)
        m2 = jnp.concatenate([-d_im, d_im], axis=1)
        shifted = jnp.where(chunk_in_seq >= dist, pltpu.roll(x, dist, 0), 0.0)
        x = x + shifted * m1 + pltpu.roll(shifted, P, 1) * m2
        dist *= 2
    x_in = jnp.where(chunk_in_seq >= 1, pltpu.roll(x, 1, 0), 0.0)

    xin_ref[...] = x_in.astype(BF16)

    srow = lax.broadcasted_iota(I32, (L, L), 0)
    tcol = lax.broadcasted_iota(I32, (L, L), 1)
    keep = tcol >= srow

    def toeplitz(o, i):
        k_row = jnp.broadcast_to(kt_ref[pl.ds(o * G + i, 1), :], (L, L))
        return jnp.where(keep, pltpu.roll(k_row, 0, 1, stride=1, stride_axis=0), 0.0).astype(BF16)

    def build(op, slot):
        for i in range(G):
            for oo in range(2):
                toe_ref[slot, i * L:(i + 1) * L, oo * L:(oo + 1) * L] = toeplitz(2 * op + oo, i)

    def emit(op, slot):
        acc_ref[op] = (jnp.dot(xin_ref[...], r_ref[op], preferred_element_type=F32)
                       + jnp.dot(a_ref[...], toe_ref[slot], preferred_element_type=F32))

    n_op = G // 2
    build(0, 0)

    def two_pairs(q, _):
        build(2 * q + 1, 1)
        emit(2 * q, 0)
        build(jnp.minimum(2 * q + 2, n_op - 1), 0)
        emit(2 * q + 1, 1)
        return 0

    lax.fori_loop(0, n_op // 2, two_pairs, 0)

    for o in range(G):
        y_ref[o // SUBLANES, pl.ds(o % SUBLANES, nct, stride=SUBLANES), :] = (
            acc_ref[o // 2, :, (o % 2) * L:(o % 2 + 1) * L])


def _s5_core(u4, lam_re, lam_im, log_dt, b_re, b_im, c_re, c_im, bn, seq):
    G, P, L = S5_GROUP, S5_STATE, CHUNK
    ng = S5_GROUPS
    nc = seq // L
    nct = bn * nc
    rt = G // SUBLANES
    u3 = u4.reshape(D_MODEL // SUBLANES, nct * SUBLANES, L)
    lam_r = jnp.stack([lam_re, lam_im], axis=1)
    lam_c = jnp.stack([lam_re, lam_im], axis=2)
    brt = jnp.swapaxes(b_re, 1, 2)
    bit = jnp.swapaxes(b_im, 1, 2)
    spec = lambda a: pl.BlockSpec((None,) + a.shape[1:], lambda g: (g,) + (0,) * (a.ndim - 1))
    params = [
        lam_r, lam_c, log_dt.reshape(ng, 1, 1),
        jnp.repeat(c_re, G, axis=1), jnp.repeat(c_im, G, axis=1),
        brt, bit,
        jnp.tile(brt, (1, G, 1)), jnp.tile(bit, (1, G, 1)),
        jnp.swapaxes(c_re, 1, 2), jnp.swapaxes(c_im, 1, 2),
    ]
    y3 = pl.pallas_call(
        functools.partial(_s5_kernel, nc),
        grid=(ng,),
        in_specs=[pl.BlockSpec((rt, nct * SUBLANES, L), lambda g: (g, 0, 0))] + [spec(a) for a in params],
        out_specs=pl.BlockSpec((rt, nct * SUBLANES, L), lambda g: (g, 0, 0)),
        out_shape=jax.ShapeDtypeStruct(u3.shape, F32),
        scratch_shapes=[
            pltpu.VMEM((G // 2, nct, 2 * L), F32),
            pltpu.VMEM((G * G, L), F32),
            pltpu.VMEM((G // 2, 2 * P, 2 * L), BF16),
            pltpu.VMEM((2, G * L, 2 * L), BF16),
            pltpu.VMEM((nct, G * L), BF16),
            pltpu.VMEM((nct, 2 * P), BF16),
        ],
        compiler_params=_cparams(1),
        name="s5_core",
    )(u3, *params)
    return u3, y3


def _s5_out_kernel(u_ref, y_ref, dsk_ref, x_ref, g_ref, w_ref, lng_ref, lnb_ref, o_ref):
    nch = u_ref.shape[1] // SUBLANES
    for cl in range(nch):
        rows = slice(cl * SUBLANES, (cl + 1) * SUBLANES)
        v = _gelu_tanh(y_ref[:, rows, :] + dsk_ref[...] * u_ref[:, rows, :])
        vt = v.reshape(D_MODEL, LANES).T
        z = jnp.dot(vt.astype(BF16), w_ref[...], preferred_element_type=F32)
        y = z[:, :D_MODEL] * _sigmoid(z[:, D_MODEL:])
        tok = slice(cl * LANES, (cl + 1) * LANES)
        o_ref[tok, :] = _layer_norm(DN_ALPHA * x_ref[tok, :] + g_ref[...] * y, lng_ref[...], lnb_ref[...])


def _s5_out(u3, y3, d_skip, x, gate, w_glu, lng, lnb, seq, tm=512):
    t, d = x.shape
    tpb = seq // tm
    nch = tm // LANES
    dsk = jnp.broadcast_to(d_skip.reshape(d // SUBLANES, SUBLANES, 1), (d // SUBLANES, SUBLANES, LANES))
    row = pl.BlockSpec((1, d), lambda i: (0, 0))
    tile3 = pl.BlockSpec((d // SUBLANES, nch * SUBLANES, LANES), lambda i: (0, i, 0))
    return pl.pallas_call(
        _s5_out_kernel,
        grid=(t // tm,),
        in_specs=[tile3, tile3,
                  pl.BlockSpec(dsk.shape, lambda i: (0, 0, 0)),
                  pl.BlockSpec((tm, d), lambda i: (i, 0)),
                  pl.BlockSpec((None, 1, d), lambda i: (i // tpb, 0, 0)),
                  pl.BlockSpec(w_glu.shape, lambda i: (0, 0)),
                  row, row],
        out_specs=pl.BlockSpec((tm, d), lambda i: (i, 0)),
        out_shape=jax.ShapeDtypeStruct((t, d), F32),
        compiler_params=_cparams(1),
        name="s5_out",
    )(u3, y3, dsk, x, gate, w_glu, lng.reshape(1, d), lnb.reshape(1, d))


def _s5_layer(x, mod, w_in, lam_re, lam_im, log_dt, b_re, b_im, c_re, c_im, d_skip, w_glu, lng, lnb, bn, seq):
    sh, sc, gate = mod
    u4 = _s5_in(x, sc, sh, w_in.T.astype(BF16), seq)
    u3, y3 = _s5_core(u4, lam_re, lam_im, log_dt, b_re, b_im, c_re, c_im, bn, seq)
    return _s5_out(u3, y3, d_skip, x, gate, w_glu.astype(BF16), lng, lnb, seq)


def _swa_kernel(sink_ref, q_ref, kvc_ref, kvp_ref, o_ref):
    L = WINDOW
    n = pl.program_id(1)
    qi = lax.broadcasted_iota(I32, (L, 2 * L), 0)
    kj = lax.broadcasted_iota(I32, (L, 2 * L), 1)
    kmin = jnp.where(n > 0, 0, L)
    valid = (kj > qi) & (kj <= qi + L) & (kj >= kmin)
    nkv = A_KV_HEADS * A_HEAD_DIM
    for h in range(A_KV_HEADS):
        ks = slice(h * A_HEAD_DIM, (h + 1) * A_HEAD_DIM)
        vs = slice(nkv + h * A_HEAD_DIM, nkv + (h + 1) * A_HEAD_DIM)
        kb = jnp.concatenate([kvp_ref[:, ks], kvc_ref[:, ks]], axis=0).astype(BF16)
        vb = jnp.concatenate([kvp_ref[:, vs], kvc_ref[:, vs]], axis=0).astype(BF16)
        for g in range(A_GROUP):
            hd = h * A_GROUP + g
            cs = slice(hd * A_HEAD_DIM, (hd + 1) * A_HEAD_DIM)
            q = (q_ref[:, cs] * (A_HEAD_DIM ** -0.5)).astype(BF16)
            s = lax.dot_general(q, kb, _NT, preferred_element_type=F32)
            s = jnp.where(valid, s, -jnp.inf)
            sink = sink_ref[hd]
            m = jnp.maximum(jnp.max(s, axis=-1, keepdims=True), sink)
            p = jnp.exp(s - m)
            den = jnp.sum(p, axis=-1, keepdims=True) + jnp.exp(sink - m)
            o_ref[:, cs] = jnp.dot(p.astype(BF16), vb, preferred_element_type=F32) / den


def _swa_core(qkv, sinks, bn, seq):
    L = WINDOW
    nb = seq // L
    nq = A_HEADS * A_HEAD_DIM
    nkv2 = 2 * A_KV_HEADS * A_HEAD_DIM
    kvcol = nq // nkv2
    return pl.pallas_call(
        _swa_kernel,
        grid_spec=pltpu.PrefetchScalarGridSpec(
            num_scalar_prefetch=1,
            grid=(bn, nb),
            in_specs=[
                pl.BlockSpec((L, nq), lambda b, n, s: (b * nb + n, 0)),
                pl.BlockSpec((L, nkv2), lambda b, n, s: (b * nb + n, kvcol)),
                pl.BlockSpec((L, nkv2), lambda b, n, s: (b * nb + jnp.maximum(n - 1, 0), kvcol)),
            ],
            out_specs=pl.BlockSpec((L, nq), lambda b, n, s: (b * nb + n, 0)),
        ),
        out_shape=jax.ShapeDtypeStruct((bn * seq, nq), F32),
        compiler_params=_cparams(2),
        name="swa",
    )(sinks, qkv, qkv, qkv)


def _swa_layer(x, mod, w_qkv, b_qkv, sinks, w_o, lng, lnb, bn, seq):
    sh, sc, gate = mod
    (qkv,) = _proj(x, sc, sh, [w_qkv.astype(BF16)], [b_qkv.reshape(1, -1)], [False], seq)
    o = _swa_core(qkv, sinks, bn, seq)
    return _lin_ln(o, x, gate, w_o.astype(BF16), lng, lnb, seq)


ROUTER_OFF = SUBLANES


def _router_kernel(x_ref, sc_ref, sh_ref, w_ref, b_ref, e_ref, rank_ref, wcol_ref, cnt_ref, carry_ref):
    tm = x_ref.shape[0]
    i = pl.program_id(0)

    @pl.when(i == 0)
    def _():
        carry_ref[...] = jnp.zeros_like(carry_ref)

    h = x_ref[...] * (1.0 + sc_ref[...]) + sh_ref[...]
    logits = _dot3(h, w_ref[...]) + b_ref[...]
    lt = logits.T
    gl = [lt[k:k + 1, :] for k in range(E_GROUPS)]
    gmax = jnp.maximum(jnp.maximum(gl[0], gl[1]), jnp.maximum(gl[2], gl[3]))
    g_sel = jnp.where(gl[0] == gmax, 0, jnp.where(gl[1] == gmax, 1, jnp.where(gl[2] == gmax, 2, 3)))
    p_g = 1.0 / (jnp.exp(gl[0] - gmax) + jnp.exp(gl[1] - gmax) + jnp.exp(gl[2] - gmax) + jnp.exp(gl[3] - gmax))
    el = [lt[ROUTER_OFF + k * E_PER_GROUP:ROUTER_OFF + (k + 1) * E_PER_GROUP, :] for k in range(E_GROUPS)]
    eg = jnp.where(g_sel == 0, el[0], jnp.where(g_sel == 1, el[1], jnp.where(g_sel == 2, el[2], el[3])))
    sub = lax.broadcasted_iota(I32, (E_PER_GROUP, tm), 0).astype(F32)
    v1 = jnp.max(eg, axis=0, keepdims=True)
    i1 = jnp.min(jnp.where(eg == v1, sub, float(E_PER_GROUP)), axis=0, keepdims=True)
    eg2 = jnp.where(sub == i1, -jnp.inf, eg)
    v2 = jnp.max(eg2, axis=0, keepdims=True)
    i2 = jnp.min(jnp.where(eg2 == v2, sub, float(E_PER_GROUP)), axis=0, keepdims=True)
    t2 = jnp.exp(v2 - v1)
    w1 = p_g / (1.0 + t2)
    w2 = w1 * t2
    e1 = g_sel * E_PER_GROUP + i1.astype(I32)
    e2 = g_sel * E_PER_GROUP + i2.astype(I32)

    sub8 = lax.broadcasted_iota(I32, (SUBLANES, tm), 0)
    e_ref[...] = jnp.where(sub8 == 0, e1, jnp.where(sub8 == 1, e2, 0))
    subl = lax.broadcasted_iota(I32, (LANES, tm), 0)
    wcol_ref[...] = jnp.where(subl == 0, w1, jnp.where(subl == 1, w2, 0.0)).T

    eid = lax.broadcasted_iota(I32, (N_EXPERTS, tm), 0)
    oh1 = eid == e1
    oh2 = eid == e2
    ohs = jnp.where(oh1 | oh2, 1.0, 0.0)
    upper = (lax.broadcasted_iota(I32, (tm, tm), 0) <= lax.broadcasted_iota(I32, (tm, tm), 1)).astype(BF16)
    incl = jnp.dot(ohs.astype(BF16), upper, preferred_element_type=F32)
    base = incl - ohs + carry_ref[:, 0:1]
    r1 = jnp.sum(jnp.where(oh1, base, 0.0), axis=0, keepdims=True)
    r2 = jnp.sum(jnp.where(oh2, base, 0.0), axis=0, keepdims=True)
    rank_ref[...] = jnp.where(sub8 == 0, r1, jnp.where(sub8 == 1, r2, 0.0)).astype(I32)
    carry_ref[...] = carry_ref[...] + jnp.sum(ohs, axis=1, keepdims=True)
    cnt_ref[...] = carry_ref[...]


def _router(x, sc, sh, w_group, b_group, w_router, b_router, seq, tm=512):
    t, d = x.shape
    tpb = seq // tm
    w = jnp.zeros((d, LANES), F32).at[:, :E_GROUPS].set(w_group).at[:, ROUTER_OFF:ROUTER_OFF + N_EXPERTS].set(w_router)
    b = jnp.zeros((1, LANES), F32).at[0, :E_GROUPS].set(b_group).at[0, ROUTER_OFF:ROUTER_OFF + N_EXPERTS].set(b_router)
    vec = pl.BlockSpec((None, 1, d), lambda i: (i // tpb, 0, 0))
    rows = pl.BlockSpec((SUBLANES, tm), lambda i: (0, i))
    return pl.pallas_call(
        _router_kernel,
        grid=(t // tm,),
        in_specs=[pl.BlockSpec((tm, d), lambda i: (i, 0)), vec, vec,
                  pl.BlockSpec((d, LANES), lambda i: (0, 0)), pl.BlockSpec((1, LANES), lambda i: (0, 0))],
        out_specs=[rows, rows, pl.BlockSpec((tm, LANES), lambda i: (i, 0)),
                   pl.BlockSpec((N_EXPERTS, LANES), lambda i: (0, 0))],
        out_shape=[jax.ShapeDtypeStruct((SUBLANES, t), I32), jax.ShapeDtypeStruct((SUBLANES, t), I32),
                   jax.ShapeDtypeStruct((t, LANES), F32), jax.ShapeDtypeStruct((N_EXPERTS, LANES), F32)],
        scratch_shapes=[pltpu.VMEM((N_EXPERTS, LANES), F32)],
        compiler_params=_cparams(1),
        name="router",
    )(x, sc, sh, w, b)


def _pos_kernel(e_ref, rank_ref, pstart_ref, pos_ref):
    tm = e_ref.shape[1]
    eid = lax.broadcasted_iota(I32, (N_EXPERTS, tm), 0)
    start = pstart_ref[:, 0:1]
    rows = []
    for k in range(TOP_K):
        base = jnp.sum(jnp.where(eid == e_ref[k:k + 1, :], start, 0.0), axis=0, keepdims=True)
        rows.append(base.astype(I32) + rank_ref[k:k + 1, :])
    sub8 = lax.broadcasted_iota(I32, (SUBLANES, tm), 0)
    pos_ref[...] = jnp.where(sub8 == 0, rows[0], jnp.where(sub8 == 1, rows[1], 0))


def _positions(e, rank, pstart, tm=2048):
    t = e.shape[1]
    tm = min(tm, t)
    rows = pl.BlockSpec((SUBLANES, tm), lambda i: (0, i))
    return pl.pallas_call(
        _pos_kernel,
        grid=(t // tm,),
        in_specs=[rows, rows, pl.BlockSpec((N_EXPERTS, LANES), lambda i: (0, 0))],
        out_specs=rows,
        out_shape=jax.ShapeDtypeStruct((SUBLANES, t), I32),
        compiler_params=_cparams(1),
        name="moe_pos",
    )(e, rank, jnp.broadcast_to(pstart.astype(F32)[:, None], (N_EXPERTS, LANES)))


SLAB = D_MODEL // LANES


def _to_slabs(ref, val):
    rows = val.shape[0]
    for cidx in range(SLAB):
        ref[pl.ds(cidx, rows, stride=SLAB), :] = val[:, cidx * LANES:(cidx + 1) * LANES]


def _from_slabs(ref, rows):
    return jnp.concatenate([ref[pl.ds(cidx, rows, stride=SLAB), :] for cidx in range(SLAB)], axis=1)


def _row_copy(src, src_row, dst, dst_row, sem):
    s0 = pl.multiple_of(src_row * SLAB, SLAB)
    d0 = pl.multiple_of(dst_row * SLAB, SLAB)
    return pltpu.make_async_copy(src.at[pl.ds(s0, SLAB), :], dst.at[pl.ds(d0, SLAB), :], sem)


def _dispatch_kernel(pend_ref, x_ref, sc_ref, sh_ref, pos_ref, buf_ref, hbuf, zbuf, sem, zsem):
    tm = x_ref.shape[0]
    i = pl.program_id(0)

    nt = pl.num_programs(0)
    slot = i % 2

    def zero_block(j):
        start = pl.multiple_of(j * (MOE_BM * SLAB), MOE_BM * SLAB)
        return pltpu.make_async_copy(zbuf, buf_ref.at[pl.ds(start, MOE_BM * SLAB), :], zsem)

    @pl.when(i == 0)
    def _():
        zbuf[...] = jnp.zeros_like(zbuf)

        def pad_block(e, go):
            end = pend_ref[e] // MOE_BM
            prev = jnp.where(e > 0, pend_ref[jnp.maximum(e - 1, 0)] // MOE_BM, 0)

            @pl.when(end > prev)
            def _():
                cp = zero_block(end - 1)
                cp.start() if go else cp.wait()
            return 0

        lax.fori_loop(0, N_EXPERTS, lambda e, c: pad_block(e, True), 0)
        lax.fori_loop(0, N_EXPERTS, lambda e, c: pad_block(e, False), 0)
        first_unused = pend_ref[N_EXPERTS - 1] // MOE_BM
        n_blocks = buf_ref.shape[0] // (MOE_BM * SLAB)
        lax.fori_loop(first_unused, n_blocks, lambda j, c: (zero_block(j).start(), c)[1], 0)
        lax.fori_loop(first_unused, n_blocks, lambda j, c: (zero_block(j).wait(), c)[1], 0)

    _to_slabs(hbuf.at[slot], x_ref[...] * (1.0 + sc_ref[...]) + sh_ref[...])

    def issue(rb, _):
        for u in range(SUBLANES):
            r = rb * SUBLANES + u
            for k in range(TOP_K):
                _row_copy(hbuf.at[slot], r, buf_ref, pos_ref[r * TOP_K + k], sem.at[slot]).start(priority=k)
        return 0

    lax.fori_loop(0, tm // SUBLANES, issue, 0)

    def drain(s):
        for k in range(TOP_K):
            pltpu.make_async_copy(hbuf.at[s], buf_ref.at[pl.ds(0, tm * SLAB), :], sem.at[s]).wait()

    @pl.when(i > 0)
    def _():
        drain(1 - slot)

    @pl.when(i == nt - 1)
    def _():
        drain(slot)


def _flat_positions(pos):
    return pos[:TOP_K].T.reshape(-1)


def _dispatch(x, sc, sh, pos, pend, n_rows, seq, tm=512):
    t, d = x.shape
    tpb = seq // tm
    nt = t // tm
    pos3 = _flat_positions(pos)
    vec = pl.BlockSpec((None, 1, d), lambda i, p: (i // tpb, 0, 0))
    return pl.pallas_call(
        _dispatch_kernel,
        grid_spec=pltpu.PrefetchScalarGridSpec(
            num_scalar_prefetch=1,
            grid=(nt,),
            in_specs=[pl.BlockSpec((tm, d), lambda i, p: (i, 0)), vec, vec,
                      pl.BlockSpec((TOP_K * tm,), lambda i, p: (i,), memory_space=pltpu.SMEM)],
            out_specs=pl.BlockSpec(memory_space=pl.ANY),
            scratch_shapes=[pltpu.VMEM((2, tm * SLAB, LANES), F32), pltpu.VMEM((MOE_BM * SLAB, LANES), F32),
                            pltpu.SemaphoreType.DMA((2,)), pltpu.SemaphoreType.DMA(())],
        ),
        out_shape=jax.ShapeDtypeStruct((n_rows * SLAB, LANES), F32),
        compiler_params=_cparams(1),
        name="moe_dispatch",
    )(pend, x, sc, sh, pos3)


def _experts_kernel(blk_e_ref, nb_ref, x_ref, w1_ref, w3_ref, w2_ref, y_ref, wb1, wb3, wb2):
    j = pl.program_id(0)

    @pl.when((j == 0) | (blk_e_ref[j] != blk_e_ref[jnp.maximum(j - 1, 0)]))
    def _():
        wb1[...] = w1_ref[...].astype(BF16)
        wb3[...] = w3_ref[...].astype(BF16)
        wb2[...] = w2_ref[...].astype(BF16)

    @pl.when(j < nb_ref[0])
    def _():
        xb = _from_slabs(x_ref, MOE_BM).astype(BF16)
        h1 = jnp.dot(xb, wb1[...], preferred_element_type=F32)
        h3 = jnp.dot(xb, wb3[...], preferred_element_type=F32)
        a = (_silu(h1) * h3).astype(BF16)
        _to_slabs(y_ref, jnp.dot(a, wb2[...], preferred_element_type=F32))

    @pl.when(j >= nb_ref[0])
    def _():
        y_ref[...] = jnp.zeros_like(y_ref)


def _experts(buf, blk_e, nb_used, w1, w3, w2, layer):
    nblk = buf.shape[0] // (MOE_BM * SLAB)
    d, de = w1.shape[2], w1.shape[3]
    row_blk = lambda j, be, nb: (jnp.minimum(j, nb[0] - 1), 0)
    wsel = lambda j, be, nb: (layer, be[j], 0, 0)
    return pl.pallas_call(
        _experts_kernel,
        grid_spec=pltpu.PrefetchScalarGridSpec(
            num_scalar_prefetch=2,
            grid=(nblk,),
            in_specs=[pl.BlockSpec((MOE_BM * SLAB, LANES), row_blk),
                      pl.BlockSpec((None, None, d, de), wsel), pl.BlockSpec((None, None, d, de), wsel),
                      pl.BlockSpec((None, None, de, d), wsel)],
            out_specs=pl.BlockSpec((MOE_BM * SLAB, LANES), lambda j, be, nb: (j, 0)),
            scratch_shapes=[pltpu.VMEM((d, de), BF16), pltpu.VMEM((d, de), BF16), pltpu.VMEM((de, d), BF16)],
        ),
        out_shape=jax.ShapeDtypeStruct(buf.shape, F32),
        compiler_params=_cparams(1),
        name="moe_experts",
    )(blk_e, nb_used, buf, w1, w3, w2)


def _combine_kernel(pos_ref, posn_ref, y_ref, wcol_ref, x_ref, g_ref, lng_ref, lnb_ref, o_ref, ybuf, sem):
    tm = x_ref.shape[0]
    i = pl.program_id(0)
    nt = pl.num_programs(0)
    slot = i % 2

    def gather(p_ref, s):
        def issue(rb, _):
            for u in range(SUBLANES):
                r = rb * SUBLANES + u
                for k in range(TOP_K):
                    _row_copy(y_ref, p_ref[r * TOP_K + k], ybuf.at[s, k], r, sem.at[s]).start(priority=k)
            return 0

        lax.fori_loop(0, tm // SUBLANES, issue, 0)

    @pl.when(i == 0)
    def _():
        gather(pos_ref, 0)

    @pl.when(i + 1 < nt)
    def _():
        gather(posn_ref, 1 - slot)

    for k in range(TOP_K):
        pltpu.make_async_copy(y_ref.at[pl.ds(0, tm * SLAB), :], ybuf.at[slot, k], sem.at[slot]).wait()
    y = (wcol_ref[:, 0:1] * _from_slabs(ybuf.at[slot, 0], tm)
         + wcol_ref[:, 1:2] * _from_slabs(ybuf.at[slot, 1], tm))
    o_ref[...] = _layer_norm(DN_ALPHA * x_ref[...] + g_ref[...] * y, lng_ref[...], lnb_ref[...])


def _combine(y, pos, wcol, x, gate, lng, lnb, seq, tm=512):
    t, d = x.shape
    tpb = seq // tm
    nt = t // tm
    pos3 = _flat_positions(pos)
    row = pl.BlockSpec((1, d), lambda i: (0, 0))
    return pl.pallas_call(
        _combine_kernel,
        grid=(nt,),
        in_specs=[pl.BlockSpec((TOP_K * tm,), lambda i: (i,), memory_space=pltpu.SMEM),
                  pl.BlockSpec((TOP_K * tm,), lambda i: (jnp.minimum(i + 1, nt - 1),), memory_space=pltpu.SMEM),
                  pl.BlockSpec(memory_space=pl.ANY),
                  pl.BlockSpec((tm, LANES), lambda i: (i, 0)),
                  pl.BlockSpec((tm, d), lambda i: (i, 0)),
                  pl.BlockSpec((None, 1, d), lambda i: (i // tpb, 0, 0)),
                  row, row],
        out_specs=pl.BlockSpec((tm, d), lambda i: (i, 0)),
        out_shape=jax.ShapeDtypeStruct((t, d), F32),
        scratch_shapes=[pltpu.VMEM((2, TOP_K, tm * SLAB, LANES), F32), pltpu.SemaphoreType.DMA((2,))],
        compiler_params=_cparams(1),
        name="moe_combine",
    )(pos3, pos3, y, wcol, x, gate, lng.reshape(1, d), lnb.reshape(1, d))


def _moe_layer(x, mod, w_group, b_group, w_router, b_router, w1, w3, w2, layer, lng, lnb, seq):
    sh, sc, gate = mod
    t, d = x.shape
    e, rank, wcol, cnt = _router(x, sc, sh, w_group, b_group, w_router, b_router, seq)
    counts = cnt[:, 0].astype(I32)
    padded = (counts + MOE_BM - 1) // MOE_BM * MOE_BM
    pend = jnp.cumsum(padded)
    pstart = pend - padded
    n_rows = t * TOP_K + N_EXPERTS * MOE_BM
    nblk = n_rows // MOE_BM
    blk_start = jnp.arange(nblk, dtype=I32) * MOE_BM
    blk_e = jnp.minimum(jnp.sum(pend[None, :] <= blk_start[:, None], axis=1), N_EXPERTS - 1).astype(I32)
    nb_used = (pend[-1:] // MOE_BM).astype(I32)
    pos = _positions(e, rank, pstart)
    buf = _dispatch(x, sc, sh, pos, pend.astype(I32), n_rows, seq)
    y = _experts(buf, blk_e, nb_used, w1, w3, w2, layer)
    return _combine(y, pos, wcol, x, gate, lng, lnb, seq)


def kernel(x, c, ada_w, ada_b, ln1_g, ln1_b, ln2_g, ln2_b, mlstm_w_in, mlstm_conv_w, mlstm_conv_b, mlstm_b_if, mlstm_norm_g, mlstm_w_out, s5_w_in, s5_lam_re, s5_lam_im, s5_log_dt, s5_b_re, s5_b_im, s5_c_re, s5_c_im, s5_d, s5_w_glu, swa_w_qkv, swa_b_qkv, swa_sinks, swa_w_o, moe_w_group, moe_b_group, moe_w_router, moe_b_router, moe_w1, moe_w3, moe_w2):
    bn, seq, d = x.shape
    depth = ada_w.shape[0]
    mods = _ada(c, ada_w, ada_b).reshape(depth, bn, 6, 1, d)
    xt = x.reshape(bn * seq, d)
    for i in range(depth):
        m1 = tuple(mods[i, :, k] for k in range(3))
        m2 = tuple(mods[i, :, k] for k in range(3, 6))
        kind, j = i % N_MIXERS, i // N_MIXERS
        if kind == 0:
            xt = _mlstm_layer(xt, m1, mlstm_w_in[j], mlstm_conv_w[j], mlstm_conv_b[j], mlstm_b_if[j],
                              mlstm_norm_g[j], mlstm_w_out[j], ln1_g[i], ln1_b[i], bn, seq)
        elif kind == 1:
            xt = _s5_layer(xt, m1, s5_w_in[j], s5_lam_re[j], s5_lam_im[j], s5_log_dt[j], s5_b_re[j], s5_b_im[j],
                           s5_c_re[j], s5_c_im[j], s5_d[j], s5_w_glu[j], ln1_g[i], ln1_b[i], bn, seq)
        else:
            xt = _swa_layer(xt, m1, swa_w_qkv[j], swa_b_qkv[j], swa_sinks[j], swa_w_o[j],
                            ln1_g[i], ln1_b[i], bn, seq)
        xt = _moe_layer(xt, m2, moe_w_group[i], moe_b_group[i], moe_w_router[i], moe_b_router[i],
                        moe_w1, moe_w3, moe_w2, i, ln2_g[i], ln2_b[i], seq)
    return xt.reshape(bn, seq, d)
```

```python
import functools
import math

import jax
import jax.numpy as jnp
from jax import lax
from jax.experimental import pallas as pl
from jax.experimental.pallas import tpu as pltpu

F32 = jnp.float32
BF16 = jnp.bfloat16
I32 = jnp.int32

D_MODEL = 1024
DEPTH = 4
N_MIXERS = 3
DN_ALPHA = (2 * DEPTH) ** 0.25
LN_EPS = 1e-5

M_HEADS = 4
M_DQK = D_MODEL // 8
M_DV = D_MODEL // M_HEADS
M_CONV = 4
N_QK = 2 * M_HEADS * M_DQK
N_V = M_HEADS * M_DV

S5_GROUP = 16
S5_GROUPS = D_MODEL // S5_GROUP
S5_STATE = 64

A_HEADS = 16
A_KV_HEADS = 4
A_GROUP = A_HEADS // A_KV_HEADS
A_HEAD_DIM = D_MODEL // A_HEADS
WINDOW = 128

E_GROUPS = 4
E_PER_GROUP = 8
N_EXPERTS = E_GROUPS * E_PER_GROUP
TOP_K = 2
D_EXPERT = 512

LANES = 128
SUBLANES = 8
CHUNK = LANES
MOE_BM = 512
VMEM_LIMIT = 56 * 1024 * 1024


def _cparams(n_axes):
    return pltpu.CompilerParams(dimension_semantics=("arbitrary",) * n_axes, vmem_limit_bytes=VMEM_LIMIT)


def _split_bf16(a):
    hi = a.astype(BF16)
    lo = (a - hi.astype(F32)).astype(BF16)
    return hi, lo


def _dot3(a, b, dims=None):
    ah, al = _split_bf16(a)
    bh, bl = _split_bf16(b)
    if dims is None:
        dot = lambda x, y: jnp.dot(x, y, preferred_element_type=F32)
    else:
        dot = lambda x, y: lax.dot_general(x, y, dims, preferred_element_type=F32)
    return dot(ah, bh) + (dot(ah, bl) + dot(al, bh))


def _bdot(a, b):
    return jnp.dot(a.astype(BF16), b.astype(BF16), preferred_element_type=F32)


_NT = (((1,), (1,)), ((), ()))
_TN = (((0,), (0,)), ((), ()))


def _layer_norm(v, g, b):
    mu = jnp.mean(v, axis=-1, keepdims=True)
    d = v - mu
    var = jnp.mean(d * d, axis=-1, keepdims=True)
    return d * lax.rsqrt(var + LN_EPS) * g + b


def _sigmoid(x):
    return 1.0 / (1.0 + jnp.exp(-x))


def _silu(x):
    return x * _sigmoid(x)


def _log_sigmoid(x):
    return jnp.minimum(x, 0.0) - jnp.log1p(jnp.exp(-jnp.abs(x)))


def _gelu_tanh(x):
    return 0.5 * x * (1.0 + jnp.tanh(math.sqrt(2.0 / math.pi) * (x + 0.044715 * (x * x * x))))


def _ada_kernel(c_ref, w_ref, b_ref, o_ref):
    cond = _silu(c_ref[...])
    o_ref[...] = _dot3(cond, w_ref[...]) + b_ref[...]


def _ada(c, ada_w, ada_b):
    bn, d = c.shape
    depth, _, n = ada_w.shape
    tn = 1536
    c8 = jnp.zeros((SUBLANES, d), F32).at[:bn].set(c)
    out = pl.pallas_call(
        _ada_kernel,
        grid=(depth, n // tn),
        in_specs=[
            pl.BlockSpec((SUBLANES, d), lambda l, j: (0, 0)),
            pl.BlockSpec((None, d, tn), lambda l, j: (l, 0, j)),
            pl.BlockSpec((None, 1, tn), lambda l, j: (l, 0, j)),
        ],
        out_specs=pl.BlockSpec((None, SUBLANES, tn), lambda l, j: (l, 0, j)),
        out_shape=jax.ShapeDtypeStruct((depth, SUBLANES, n), F32),
        compiler_params=_cparams(2),
        name="ada",
    )(c8, ada_w, ada_b.reshape(depth, 1, n))
    return out[:, :bn]


def _proj_kernel(precise, x_ref, sc_ref, sh_ref, *refs):
    n = len(precise)
    w_refs, b_refs, o_refs = refs[:n], refs[n:2 * n], refs[2 * n:]
    h = x_ref[...] * (1.0 + sc_ref[...]) + sh_ref[...]
    hb = h.astype(BF16)
    for w_ref, b_ref, o_ref, p in zip(w_refs, b_refs, o_refs, precise):
        if p:
            r = _dot3(h, w_ref[...])
        else:
            r = jnp.dot(hb, w_ref[...], preferred_element_type=F32)
        o_ref[...] = (r + b_ref[...]).astype(o_ref.dtype)


def _proj(x, sc, sh, ws, bs, precise, seq, out_dtypes=None, tm=512):
    out_dtypes = out_dtypes or [F32] * len(ws)
    t, d = x.shape
    tpb = seq // tm
    vec = pl.BlockSpec((None, 1, d), lambda i: (i // tpb, 0, 0))
    in_specs = [pl.BlockSpec((tm, d), lambda i: (i, 0)), vec, vec]
    in_specs += [pl.BlockSpec(w.shape, lambda i: (0, 0)) for w in ws]
    in_specs += [pl.BlockSpec(b.shape, lambda i: (0, 0)) for b in bs]
    return pl.pallas_call(
        functools.partial(_proj_kernel, tuple(precise)),
        grid=(t // tm,),
        in_specs=in_specs,
        out_specs=[pl.BlockSpec((tm, w.shape[1]), lambda i: (i, 0)) for w in ws],
        out_shape=[jax.ShapeDtypeStruct((t, w.shape[1]), dt) for w, dt in zip(ws, out_dtypes)],
        compiler_params=_cparams(1),
        name="proj",
    )(x, sc, sh, *ws, *bs)


def _lin_ln_kernel(a_ref, x_ref, g_ref, w_ref, lng_ref, lnb_ref, o_ref):
    y = jnp.dot(a_ref[...].astype(BF16), w_ref[...], preferred_element_type=F32)
    o_ref[...] = _layer_norm(DN_ALPHA * x_ref[...] + g_ref[...] * y, lng_ref[...], lnb_ref[...])


def _lin_ln(a, x, gate, w, lng, lnb, seq, tm=512):
    t, d = x.shape
    k = a.shape[1]
    tpb = seq // tm
    row = pl.BlockSpec((1, d), lambda i: (0, 0))
    return pl.pallas_call(
        _lin_ln_kernel,
        grid=(t // tm,),
        in_specs=[
            pl.BlockSpec((tm, k), lambda i: (i, 0)),
            pl.BlockSpec((tm, d), lambda i: (i, 0)),
            pl.BlockSpec((None, 1, d), lambda i: (i // tpb, 0, 0)),
            pl.BlockSpec((k, d), lambda i: (0, 0)),
            row, row,
        ],
        out_specs=pl.BlockSpec((tm, d), lambda i: (i, 0)),
        out_shape=jax.ShapeDtypeStruct((t, d), F32),
        compiler_params=_cparams(1),
        name="lin_ln",
    )(a, x, gate, w, lng.reshape(1, d), lnb.reshape(1, d))


def _mlstm_kernel(qk_ref, v_ref, o_ref, gates_ref, cw_ref, cb_ref, ng_ref, out_ref,
                  qkbuf, c_ref, m_ref):
    @pl.when(pl.program_id(0) == 0)
    def _():
        qkbuf[...] = jnp.zeros_like(qkbuf)
        c_ref[...] = jnp.zeros_like(c_ref)
        m_ref[...] = jnp.zeros_like(m_ref)

    for b in range(qk_ref.shape[0]):
        _mlstm_chunk(qk_ref.at[b], v_ref.at[b], o_ref.at[b], gates_ref.at[b], cw_ref, cb_ref, ng_ref,
                     out_ref.at[b], qkbuf.at[b], c_ref.at[b], m_ref.at[b])


def _mlstm_chunk(qk_ref, v_ref, o_ref, gates_ref, cw_ref, cb_ref, ng_ref, out_ref, qkbuf, c_ref, m_ref):
    L = CHUNK
    qkbuf[SUBLANES:SUBLANES + L, :] = qk_ref[...]
    conv = cb_ref[...] + cw_ref[0:1, :] * qkbuf[pl.ds(SUBLANES - 3, L), :]
    for j in range(1, M_CONV):
        conv = conv + cw_ref[j:j + 1, :] * qkbuf[pl.ds(SUBLANES - 3 + j, L), :]
    qk = _silu(conv)
    qkbuf[0:SUBLANES, :] = qkbuf[L:L + SUBLANES, :]

    gates = gates_ref[...]
    gates_t = gates.T
    lf_col = _log_sigmoid(gates)
    lf_row = _log_sigmoid(gates_t[0:2 * SUBLANES, :])
    li_row = gates_t[0:M_HEADS, :]
    row_i = lax.broadcasted_iota(I32, (L, L), 0)
    col_i = lax.broadcasted_iota(I32, (L, L), 1)
    causal = col_i <= row_i
    lower = causal.astype(BF16)
    upper = (row_i <= col_i).astype(BF16)
    lf_ch, lf_cl = _split_bf16(lf_col)
    b_col_all = (jnp.dot(lower, lf_ch, preferred_element_type=F32)
                 + jnp.dot(lower, lf_cl, preferred_element_type=F32))
    lf_rh, lf_rl = _split_bf16(lf_row)
    b_row_all = (jnp.dot(lf_rh, upper, preferred_element_type=F32)
                 + jnp.dot(lf_rl, upper, preferred_element_type=F32))
    ones_col = (lax.broadcasted_iota(I32, (L, LANES), 1) == 0).astype(BF16)

    for h in range(M_HEADS):
        q = qk[:, h * M_DQK:(h + 1) * M_DQK].astype(BF16)
        k = qk[:, N_QK // 2 + h * M_DQK:N_QK // 2 + (h + 1) * M_DQK] * (M_DQK ** -0.5)
        vaug = jnp.concatenate([v_ref[:, h * M_DV:(h + 1) * M_DV], ones_col], axis=1)
        b_col = b_col_all[:, M_HEADS + h:M_HEADS + h + 1]
        li_col = gates[:, h:h + 1]
        b_row = b_row_all[M_HEADS + h:M_HEADS + h + 1, :]
        li_r = li_row[h:h + 1, :]
        m_prev = m_ref[h]
        m_prev1 = m_prev[:, 0:1]
        d = jnp.where(causal, b_col - b_row + li_r, -jnp.inf)
        inter = b_col + m_prev1
        m_t = jnp.maximum(inter, jnp.max(d, axis=-1, keepdims=True))
        s = lax.dot_general(q, k.astype(BF16), _NT, preferred_element_type=F32)
        w = jnp.exp(d - m_t) * s
        e_inter = jnp.exp(inter - m_t)
        c_aug = c_ref[h]
        num = (jnp.dot(w.astype(BF16), vaug, preferred_element_type=F32)
               + e_inter * jnp.dot(q, c_aug.astype(BF16), preferred_element_type=F32))
        den = num[:, M_DV:M_DV + 1]
        hh = num[:, :M_DV] / jnp.maximum(jnp.abs(den), jnp.exp(-m_t))
        mu = jnp.mean(hh, axis=-1, keepdims=True)
        dh = hh - mu
        var = jnp.mean(dh * dh, axis=-1, keepdims=True)
        hn = dh * lax.rsqrt(var + LN_EPS)
        sl = slice(h * M_DV, (h + 1) * M_DV)
        out_ref[:, sl] = (hn * ng_ref[:, sl] * _sigmoid(o_ref[:, sl].astype(F32))).astype(out_ref.dtype)

        b_last = b_col[L - 1:L, :]
        g_col = b_last - b_col + li_col
        m_new = jnp.maximum(b_last + m_prev1, jnp.max(g_col, axis=0, keepdims=True))
        decay = jnp.exp(b_last + m_prev1 - m_new)
        wk = (jnp.exp(g_col - m_new) * k).astype(BF16)
        c_ref[h] = decay * c_aug + lax.dot_general(wk, vaug, _TN, preferred_element_type=F32)
        m_ref[h] = jnp.broadcast_to(m_new, (1, LANES))


def _mlstm_core(qk, v, o, gates, conv_w, conv_b, norm_g, bn, seq):
    L = CHUNK
    nc = seq // L
    tok = lambda w: pl.BlockSpec((bn, L, w), lambda c: (0, c, 0))
    full = lambda r, w: pl.BlockSpec((r, w), lambda c: (0, 0))
    per_batch = lambda a: a.reshape(bn, seq, a.shape[-1])
    out = pl.pallas_call(
        _mlstm_kernel,
        grid=(nc,),
        in_specs=[tok(N_QK), tok(N_V), tok(D_MODEL), tok(LANES),
                  full(M_CONV, N_QK), full(1, N_QK), full(1, N_V)],
        out_specs=tok(N_V),
        out_shape=jax.ShapeDtypeStruct((bn, seq, N_V), BF16),
        scratch_shapes=[
            pltpu.VMEM((bn, L + SUBLANES, N_QK), F32),
            pltpu.VMEM((bn, M_HEADS, M_DQK, M_DV + LANES), F32),
            pltpu.VMEM((bn, M_HEADS, 1, LANES), F32),
        ],
        compiler_params=_cparams(1),
        name="mlstm",
    )(per_batch(qk), per_batch(v), per_batch(o), per_batch(gates),
      conv_w, conv_b.reshape(1, N_QK), norm_g.reshape(1, N_V))
    return out.reshape(bn * seq, N_V)


def _mlstm_layer(x, mod, w_in, conv_w, conv_b, b_if, norm_g, w_out, lng, lnb, bn, seq):
    sh, sc, gate = mod
    w_qk = w_in[:, :N_QK].astype(BF16)
    w_v = w_in[:, N_QK:N_QK + N_V].astype(BF16)
    w_o = w_in[:, N_QK + N_V:N_QK + N_V + D_MODEL].astype(BF16)
    w_g = jnp.zeros((D_MODEL, LANES), F32).at[:, :2 * M_HEADS].set(w_in[:, N_QK + N_V + D_MODEL:])
    b_g = jnp.zeros((1, LANES), F32).at[0, :2 * M_HEADS].set(b_if)
    z = lambda n: jnp.zeros((1, n), F32)
    qk, v, o, gates = _proj(x, sc, sh, [w_qk, w_v, w_o, w_g], [z(N_QK), z(N_V), z(D_MODEL), b_g],
                            [False, False, False, True], seq, out_dtypes=[F32, BF16, BF16, F32])
    hg = _mlstm_core(qk, v, o, gates, conv_w, conv_b, norm_g, bn, seq)
    return _lin_ln(hg, x, gate, w_out.astype(BF16), lng, lnb, seq)


def _s5_in_kernel(x_ref, sc_ref, sh_ref, wt_ref, o_ref):
    h = x_ref[...] * (1.0 + sc_ref[...]) + sh_ref[...]
    ut = lax.dot_general(wt_ref[...], h.astype(BF16), _NT, preferred_element_type=F32)
    for cl in range(o_ref.shape[1]):
        o_ref[:, cl, :, :] = ut[:, cl * LANES:(cl + 1) * LANES].reshape(D_MODEL // SUBLANES, SUBLANES, LANES)


def _s5_in(x, sc, sh, w_t, seq, tm=512):
    t, d = x.shape
    tpb = seq // tm
    nch = tm // LANES
    vec = pl.BlockSpec((None, 1, d), lambda i: (i // tpb, 0, 0))
    return pl.pallas_call(
        _s5_in_kernel,
        grid=(t // tm,),
        in_specs=[pl.BlockSpec((tm, d), lambda i: (i, 0)), vec, vec,
                  pl.BlockSpec((d, d), lambda i: (0, 0))],
        out_specs=pl.BlockSpec((d // SUBLANES, nch, SUBLANES, LANES), lambda i: (0, i, 0, 0)),
        out_shape=jax.ShapeDtypeStruct((d // SUBLANES, t // LANES, SUBLANES, LANES), F32),
        compiler_params=_cparams(1),
        name="s5_in",
    )(x, sc, sh, w_t)


def _cpow(lr, lim, dt, steps):
    mag = jnp.exp(lr * dt * steps)
    ang = lim * dt * steps
    return mag * jnp.cos(ang), mag * jnp.sin(ang)


def _s5_kernel(nc, u_ref, lamr_ref, lamc_ref, ldt_ref, crep_ref, cirep_ref, brt_ref, bit_ref,
               brtt_ref, bitt_ref, crt_ref, cit_ref, y_ref, acc_ref, kt_ref, r_ref, toe_ref, a_ref, xin_ref):
    L = CHUNK
    G = S5_GROUP
    P = S5_STATE
    nct = acc_ref.shape[1]

    def u_rows(i):
        return u_ref[i // SUBLANES, pl.ds(i % SUBLANES, nct, stride=SUBLANES), :]

    dt = jnp.exp(ldt_ref[...])
    lr_r, lim_r = lamr_ref[0:1, :], lamr_ref[1:2, :]
    lr_c, lim_c = lamc_ref[:, 0:1], lamc_ref[:, 1:2]

    a_re, a_im = _cpow(lr_r, lim_r, dt, 1.0)
    lam_sq = lr_r * lr_r + lim_r * lim_r
    t_re = ((a_re - 1.0) * lr_r + a_im * lim_r) / lam_sq
    t_im = (a_im * lr_r - (a_re - 1.0) * lim_r) / lam_sq

    bbt_re = t_re * brt_ref[...] - t_im * bit_ref[...]
    bbt_im = t_re * bit_ref[...] + t_im * brt_ref[...]
    bbtt_re = t_re * brtt_ref[...] - t_im * bitt_ref[...]
    bbtt_im = t_re * bitt_ref[...] + t_im * brtt_ref[...]
    cb_re = crep_ref[...] * bbtt_re - cirep_ref[...] * bbtt_im
    cb_im = crep_ref[...] * bbtt_im + cirep_ref[...] * bbtt_re

    lag = lax.broadcasted_iota(I32, (P, L), 1).astype(F32)
    p0_re, p0_im = _cpow(lr_c, lim_c, dt, lag)
    ac_re, ac_im = _cpow(lr_c, lim_c, dt, 1.0)
    p1_re = p0_re * ac_re - p0_im * ac_im
    p1_im = p0_re * ac_im + p0_im * ac_re
    kt_ref[...] = _dot3(cb_re, p0_re) - _dot3(cb_im, p0_im)

    for o in range(G):
        cr_o = crt_ref[:, o:o + 1]
        ci_o = cit_ref[:, o:o + 1]
        cols = slice((o % 2) * L, (o % 2 + 1) * L)
        r_ref[o // 2, 0:P, cols] = (cr_o * p1_re - ci_o * p1_im).astype(BF16)
        r_ref[o // 2, P:2 * P, cols] = (-(cr_o * p1_im + ci_o * p1_re)).astype(BF16)

    for i in range(G):
        a_ref[:, i * L:(i + 1) * L] = u_rows(i).astype(BF16)

    back = (L - 1.0) - lax.broadcasted_iota(I32, (L, P), 0).astype(F32)
    ps_re, ps_im = _cpow(lr_r, lim_r, dt, back)
    for i in range(G):
        bi_re = bbt_re[i:i + 1, :]
        bi_im = bbt_im[i:i + 1, :]
        w_i = jnp.concatenate([ps_re * bi_re - ps_im * bi_im, ps_re * bi_im + ps_im * bi_re], axis=1)
        toe_ref[0, i * L:(i + 1) * L, 0:2 * P] = w_i.astype(BF16)
    e_acc = jnp.dot(a_ref[...], toe_ref[0, :, 0:2 * P], preferred_element_type=F32)

    chunk_in_seq = lax.broadcasted_iota(I32, (nct, 2 * P), 0) % nc
    x = e_acc
    dist = 1
    d_re, d_im = _cpow(lr_r, lim_r, dt, float(L))
    while dist < nc:
        if dist > 1:
            d_re, d_im = d_re * d_re - d_im * d_im, 2.0 * d_re * d_im
        m1 = jnp.concatenate([d_re, d_re], axis=1)
        m2 = jnp.concatenate([-d_im, d_im], axis=1)
        shifted = jnp.where(chunk_in_seq >= dist, pltpu.roll(x, dist, 0), 0.0)
        x = x + shifted * m1 + pltpu.roll(shifted, P, 1) * m2
        dist *= 2
    x_in = jnp.where(chunk_in_seq >= 1, pltpu.roll(x, 1, 0), 0.0)

    xin_ref[...] = x_in.astype(BF16)

    srow = lax.broadcasted_iota(I32, (L, L), 0)
    tcol = lax.broadcasted_iota(I32, (L, L), 1)
    keep = tcol >= srow

    def toeplitz(o, i):
        k_row = jnp.broadcast_to(kt_ref[pl.ds(o * G + i, 1), :], (L, L))
        return jnp.where(keep, pltpu.roll(k_row, 0, 1, stride=1, stride_axis=0), 0.0).astype(BF16)

    def build(op, slot):
        for i in range(G):
            for oo in range(2):
                toe_ref[slot, i * L:(i + 1) * L, oo * L:(oo + 1) * L] = toeplitz(2 * op + oo, i)

    def emit(op, slot):
        acc_ref[op] = (jnp.dot(xin_ref[...], r_ref[op], preferred_element_type=F32)
                       + jnp.dot(a_ref[...], toe_ref[slot], preferred_element_type=F32))

    n_op = G // 2
    build(0, 0)

    def two_pairs(q, _):
        build(2 * q + 1, 1)
        emit(2 * q, 0)
        build(jnp.minimum(2 * q + 2, n_op - 1), 0)
        emit(2 * q + 1, 1)
        return 0

    lax.fori_loop(0, n_op // 2, two_pairs, 0)

    for o in range(G):
        y_ref[o // SUBLANES, pl.ds(o % SUBLANES, nct, stride=SUBLANES), :] = (
            acc_ref[o // 2, :, (o % 2) * L:(o % 2 + 1) * L])


def _s5_core(u4, lam_re, lam_im, log_dt, b_re, b_im, c_re, c_im, bn, seq):
    G, P, L = S5_GROUP, S5_STATE, CHUNK
    ng = S5_GROUPS
    nc = seq // L
    nct = bn * nc
    rt = G // SUBLANES
    u3 = u4.reshape(D_MODEL // SUBLANES, nct * SUBLANES, L)
    lam_r = jnp.stack([lam_re, lam_im], axis=1)
    lam_c = jnp.stack([lam_re, lam_im], axis=2)
    brt = jnp.swapaxes(b_re, 1, 2)
    bit = jnp.swapaxes(b_im, 1, 2)
    spec = lambda a: pl.BlockSpec((None,) + a.shape[1:], lambda g: (g,) + (0,) * (a.ndim - 1))
    params = [
        lam_r, lam_c, log_dt.reshape(ng, 1, 1),
        jnp.repeat(c_re, G, axis=1), jnp.repeat(c_im, G, axis=1),
        brt, bit,
        jnp.tile(brt, (1, G, 1)), jnp.tile(bit, (1, G, 1)),
        jnp.swapaxes(c_re, 1, 2), jnp.swapaxes(c_im, 1, 2),
    ]
    y3 = pl.pallas_call(
        functools.partial(_s5_kernel, nc),
        grid=(ng,),
        in_specs=[pl.BlockSpec((rt, nct * SUBLANES, L), lambda g: (g, 0, 0))] + [spec(a) for a in params],
        out_specs=pl.BlockSpec((rt, nct * SUBLANES, L), lambda g: (g, 0, 0)),
        out_shape=jax.ShapeDtypeStruct(u3.shape, F32),
        scratch_shapes=[
            pltpu.VMEM((G // 2, nct, 2 * L), F32),
            pltpu.VMEM((G * G, L), F32),
            pltpu.VMEM((G // 2, 2 * P, 2 * L), BF16),
            pltpu.VMEM((2, G * L, 2 * L), BF16),
            pltpu.VMEM((nct, G * L), BF16),
            pltpu.VMEM((nct, 2 * P), BF16),
        ],
        compiler_params=_cparams(1),
        name="s5_core",
    )(u3, *params)
    return u3, y3


def _s5_out_kernel(u_ref, y_ref, dsk_ref, x_ref, g_ref, w_ref, lng_ref, lnb_ref, o_ref):
    nch = u_ref.shape[1] // SUBLANES
    for cl in range(nch):
        rows = slice(cl * SUBLANES, (cl + 1) * SUBLANES)
        v = _gelu_tanh(y_ref[:, rows, :] + dsk_ref[...] * u_ref[:, rows, :])
        vt = v.reshape(D_MODEL, LANES).T
        z = jnp.dot(vt.astype(BF16), w_ref[...], preferred_element_type=F32)
        y = z[:, :D_MODEL] * _sigmoid(z[:, D_MODEL:])
        tok = slice(cl * LANES, (cl + 1) * LANES)
        o_ref[tok, :] = _layer_norm(DN_ALPHA * x_ref[tok, :] + g_ref[...] * y, lng_ref[...], lnb_ref[...])


def _s5_out(u3, y3, d_skip, x, gate, w_glu, lng, lnb, seq, tm=512):
    t, d = x.shape
    tpb = seq // tm
    nch = tm // LANES
    dsk = jnp.broadcast_to(d_skip.reshape(d // SUBLANES, SUBLANES, 1), (d // SUBLANES, SUBLANES, LANES))
    row = pl.BlockSpec((1, d), lambda i: (0, 0))
    tile3 = pl.BlockSpec((d // SUBLANES, nch * SUBLANES, LANES), lambda i: (0, i, 0))
    return pl.pallas_call(
        _s5_out_kernel,
        grid=(t // tm,),
        in_specs=[tile3, tile3,
                  pl.BlockSpec(dsk.shape, lambda i: (0, 0, 0)),
                  pl.BlockSpec((tm, d), lambda i: (i, 0)),
                  pl.BlockSpec((None, 1, d), lambda i: (i // tpb, 0, 0)),
                  pl.BlockSpec(w_glu.shape, lambda i: (0, 0)),
                  row, row],
        out_specs=pl.BlockSpec((tm, d), lambda i: (i, 0)),
        out_shape=jax.ShapeDtypeStruct((t, d), F32),
        compiler_params=_cparams(1),
        name="s5_out",
    )(u3, y3, dsk, x, gate, w_glu, lng.reshape(1, d), lnb.reshape(1, d))


def _s5_layer(x, mod, w_in, lam_re, lam_im, log_dt, b_re, b_im, c_re, c_im, d_skip, w_glu, lng, lnb, bn, seq):
    sh, sc, gate = mod
    u4 = _s5_in(x, sc, sh, w_in.T.astype(BF16), seq)
    u3, y3 = _s5_core(u4, lam_re, lam_im, log_dt, b_re, b_im, c_re, c_im, bn, seq)
    return _s5_out(u3, y3, d_skip, x, gate, w_glu.astype(BF16), lng, lnb, seq)


def _swa_kernel(sink_ref, q_ref, kvc_ref, kvp_ref, o_ref):
    L = WINDOW
    n = pl.program_id(1)
    qi = lax.broadcasted_iota(I32, (L, 2 * L), 0)
    kj = lax.broadcasted_iota(I32, (L, 2 * L), 1)
    kmin = jnp.where(n > 0, 0, L)
    valid = (kj > qi) & (kj <= qi + L) & (kj >= kmin)
    nkv = A_KV_HEADS * A_HEAD_DIM
    for h in range(A_KV_HEADS):
        ks = slice(h * A_HEAD_DIM, (h + 1) * A_HEAD_DIM)
        vs = slice(nkv + h * A_HEAD_DIM, nkv + (h + 1) * A_HEAD_DIM)
        kb = jnp.concatenate([kvp_ref[:, ks], kvc_ref[:, ks]], axis=0).astype(BF16)
        vb = jnp.concatenate([kvp_ref[:, vs], kvc_ref[:, vs]], axis=0).astype(BF16)
        for g in range(A_GROUP):
            hd = h * A_GROUP + g
            cs = slice(hd * A_HEAD_DIM, (hd + 1) * A_HEAD_DIM)
            q = (q_ref[:, cs] * (A_HEAD_DIM ** -0.5)).astype(BF16)
            s = lax.dot_general(q, kb, _NT, preferred_element_type=F32)
            s = jnp.where(valid, s, -jnp.inf)
            sink = sink_ref[hd]
            m = jnp.maximum(jnp.max(s, axis=-1, keepdims=True), sink)
            p = jnp.exp(s - m)
            den = jnp.sum(p, axis=-1, keepdims=True) + jnp.exp(sink - m)
            o_ref[:, cs] = jnp.dot(p.astype(BF16), vb, preferred_element_type=F32) / den


def _swa_core(qkv, sinks, bn, seq):
    L = WINDOW
    nb = seq // L
    nq = A_HEADS * A_HEAD_DIM
    nkv2 = 2 * A_KV_HEADS * A_HEAD_DIM
    kvcol = nq // nkv2
    return pl.pallas_call(
        _swa_kernel,
        grid_spec=pltpu.PrefetchScalarGridSpec(
            num_scalar_prefetch=1,
            grid=(bn, nb),
            in_specs=[
                pl.BlockSpec((L, nq), lambda b, n, s: (b * nb + n, 0)),
                pl.BlockSpec((L, nkv2), lambda b, n, s: (b * nb + n, kvcol)),
                pl.BlockSpec((L, nkv2), lambda b, n, s: (b * nb + jnp.maximum(n - 1, 0), kvcol)),
            ],
            out_specs=pl.BlockSpec((L, nq), lambda b, n, s: (b * nb + n, 0)),
        ),
        out_shape=jax.ShapeDtypeStruct((bn * seq, nq), F32),
        compiler_params=_cparams(2),
        name="swa",
    )(sinks, qkv, qkv, qkv)


def _swa_layer(x, mod, w_qkv, b_qkv, sinks, w_o, lng, lnb, bn, seq):
    sh, sc, gate = mod
    (qkv,) = _proj(x, sc, sh, [w_qkv.astype(BF16)], [b_qkv.reshape(1, -1)], [False], seq)
    o = _swa_core(qkv, sinks, bn, seq)
    return _lin_ln(o, x, gate, w_o.astype(BF16), lng, lnb, seq)


ROUTER_OFF = SUBLANES


def _router_kernel(x_ref, sc_ref, sh_ref, w_ref, b_ref, e_ref, rank_ref, wcol_ref, cnt_ref, carry_ref):
    tm = x_ref.shape[0]
    i = pl.program_id(0)

    @pl.when(i == 0)
    def _():
        carry_ref[...] = jnp.zeros_like(carry_ref)

    h = x_ref[...] * (1.0 + sc_ref[...]) + sh_ref[...]
    logits = _dot3(h, w_ref[...]) + b_ref[...]
    lt = logits.T
    gl = [lt[k:k + 1, :] for k in range(E_GROUPS)]
    gmax = jnp.maximum(jnp.maximum(gl[0], gl[1]), jnp.maximum(gl[2], gl[3]))
    g_sel = jnp.where(gl[0] == gmax, 0, jnp.where(gl[1] == gmax, 1, jnp.where(gl[2] == gmax, 2, 3)))
    p_g = 1.0 / (jnp.exp(gl[0] - gmax) + jnp.exp(gl[1] - gmax) + jnp.exp(gl[2] - gmax) + jnp.exp(gl[3] - gmax))
    el = [lt[ROUTER_OFF + k * E_PER_GROUP:ROUTER_OFF + (k + 1) * E_PER_GROUP, :] for k in range(E_GROUPS)]
    eg = jnp.where(g_sel == 0, el[0], jnp.where(g_sel == 1, el[1], jnp.where(g_sel == 2, el[2], el[3])))
    sub = lax.broadcasted_iota(I32, (E_PER_GROUP, tm), 0).astype(F32)
    v1 = jnp.max(eg, axis=0, keepdims=True)
    i1 = jnp.min(jnp.where(eg == v1, sub, float(E_PER_GROUP)), axis=0, keepdims=True)
    eg2 = jnp.where(sub == i1, -jnp.inf, eg)
    v2 = jnp.max(eg2, axis=0, keepdims=True)
    i2 = jnp.min(jnp.where(eg2 == v2, sub, float(E_PER_GROUP)), axis=0, keepdims=True)
    t2 = jnp.exp(v2 - v1)
    w1 = p_g / (1.0 + t2)
    w2 = w1 * t2
    e1 = g_sel * E_PER_GROUP + i1.astype(I32)
    e2 = g_sel * E_PER_GROUP + i2.astype(I32)

    sub8 = lax.broadcasted_iota(I32, (SUBLANES, tm), 0)
    e_ref[...] = jnp.where(sub8 == 0, e1, jnp.where(sub8 == 1, e2, 0))
    subl = lax.broadcasted_iota(I32, (LANES, tm), 0)
    wcol_ref[...] = jnp.where(subl == 0, w1, jnp.where(subl == 1, w2, 0.0)).T

    eid = lax.broadcasted_iota(I32, (N_EXPERTS, tm), 0)
    oh1 = eid == e1
    oh2 = eid == e2
    ohs = jnp.where(oh1 | oh2, 1.0, 0.0)
    upper = (lax.broadcasted_iota(I32, (tm, tm), 0) <= lax.broadcasted_iota(I32, (tm, tm), 1)).astype(BF16)
    incl = jnp.dot(ohs.astype(BF16), upper, preferred_element_type=F32)
    base = incl - ohs + carry_ref[:, 0:1]
    r1 = jnp.sum(jnp.where(oh1, base, 0.0), axis=0, keepdims=True)
    r2 = jnp.sum(jnp.where(oh2, base, 0.0), axis=0, keepdims=True)
    rank_ref[...] = jnp.where(sub8 == 0, r1, jnp.where(sub8 == 1, r2, 0.0)).astype(I32)
    carry_ref[...] = carry_ref[...] + jnp.sum(ohs, axis=1, keepdims=True)
    cnt_ref[...] = carry_ref[...]


def _router(x, sc, sh, w_group, b_group, w_router, b_router, seq, tm=512):
    t, d = x.shape
    tpb = seq // tm
    w = jnp.zeros((d, LANES), F32).at[:, :E_GROUPS].set(w_group).at[:, ROUTER_OFF:ROUTER_OFF + N_EXPERTS].set(w_router)
    b = jnp.zeros((1, LANES), F32).at[0, :E_GROUPS].set(b_group).at[0, ROUTER_OFF:ROUTER_OFF + N_EXPERTS].set(b_router)
    vec = pl.BlockSpec((None, 1, d), lambda i: (i // tpb, 0, 0))
    rows = pl.BlockSpec((SUBLANES, tm), lambda i: (0, i))
    return pl.pallas_call(
        _router_kernel,
        grid=(t // tm,),
        in_specs=[pl.BlockSpec((tm, d), lambda i: (i, 0)), vec, vec,
                  pl.BlockSpec((d, LANES), lambda i: (0, 0)), pl.BlockSpec((1, LANES), lambda i: (0, 0))],
        out_specs=[rows, rows, pl.BlockSpec((tm, LANES), lambda i: (i, 0)),
                   pl.BlockSpec((N_EXPERTS, LANES), lambda i: (0, 0))],
        out_shape=[jax.ShapeDtypeStruct((SUBLANES, t), I32), jax.ShapeDtypeStruct((SUBLANES, t), I32),
                   jax.ShapeDtypeStruct((t, LANES), F32), jax.ShapeDtypeStruct((N_EXPERTS, LANES), F32)],
        scratch_shapes=[pltpu.VMEM((N_EXPERTS, LANES), F32)],
        compiler_params=_cparams(1),
        name="router",
    )(x, sc, sh, w, b)


def _pos_kernel(e_ref, rank_ref, pstart_ref, pos_ref):
    tm = e_ref.shape[1]
    eid = lax.broadcasted_iota(I32, (N_EXPERTS, tm), 0)
    start = pstart_ref[:, 0:1]
    rows = []
    for k in range(TOP_K):
        base = jnp.sum(jnp.where(eid == e_ref[k:k + 1, :], start, 0.0), axis=0, keepdims=True)
        rows.append(base.astype(I32) + rank_ref[k:k + 1, :])
    sub8 = lax.broadcasted_iota(I32, (SUBLANES, tm), 0)
    pos_ref[...] = jnp.where(sub8 == 0, rows[0], jnp.where(sub8 == 1, rows[1], 0))


def _positions(e, rank, pstart, tm=2048):
    t = e.shape[1]
    tm = min(tm, t)
    rows = pl.BlockSpec((SUBLANES, tm), lambda i: (0, i))
    return pl.pallas_call(
        _pos_kernel,
        grid=(t // tm,),
        in_specs=[rows, rows, pl.BlockSpec((N_EXPERTS, LANES), lambda i: (0, 0))],
        out_specs=rows,
        out_shape=jax.ShapeDtypeStruct((SUBLANES, t), I32),
        compiler_params=_cparams(1),
        name="moe_pos",
    )(e, rank, jnp.broadcast_to(pstart.astype(F32)[:, None], (N_EXPERTS, LANES)))


SLAB = D_MODEL // LANES


def _to_slabs(ref, val):
    rows = val.shape[0]
    for cidx in range(SLAB):
        ref[pl.ds(cidx, rows, stride=SLAB), :] = val[:, cidx * LANES:(cidx + 1) * LANES]


def _from_slabs(ref, rows):
    return jnp.concatenate([ref[pl.ds(cidx, rows, stride=SLAB), :] for cidx in range(SLAB)], axis=1)


def _row_copy(src, src_row, dst, dst_row, sem):
    s0 = pl.multiple_of(src_row * SLAB, SLAB)
    d0 = pl.multiple_of(dst_row * SLAB, SLAB)
    return pltpu.make_async_copy(src.at[pl.ds(s0, SLAB), :], dst.at[pl.ds(d0, SLAB), :], sem)


def _dispatch_kernel(pend_ref, x_ref, sc_ref, sh_ref, pos_ref, buf_ref, hbuf, zbuf, sem, zsem):
    tm = x_ref.shape[0]
    i = pl.program_id(0)

    nt = pl.num_programs(0)
    slot = i % 2

    def zero_block(j):
        start = pl.multiple_of(j * (MOE_BM * SLAB), MOE_BM * SLAB)
        return pltpu.make_async_copy(zbuf, buf_ref.at[pl.ds(start, MOE_BM * SLAB), :], zsem)

    @pl.when(i == 0)
    def _():
        zbuf[...] = jnp.zeros_like(zbuf)

        def pad_block(e, go):
            end = pend_ref[e] // MOE_BM
            prev = jnp.where(e > 0, pend_ref[jnp.maximum(e - 1, 0)] // MOE_BM, 0)

            @pl.when(end > prev)
            def _():
                cp = zero_block(end - 1)
                cp.start() if go else cp.wait()
            return 0

        lax.fori_loop(0, N_EXPERTS, lambda e, c: pad_block(e, True), 0)
        lax.fori_loop(0, N_EXPERTS, lambda e, c: pad_block(e, False), 0)
        first_unused = pend_ref[N_EXPERTS - 1] // MOE_BM
        n_blocks = buf_ref.shape[0] // (MOE_BM * SLAB)
        lax.fori_loop(first_unused, n_blocks, lambda j, c: (zero_block(j).start(), c)[1], 0)
        lax.fori_loop(first_unused, n_blocks, lambda j, c: (zero_block(j).wait(), c)[1], 0)

    _to_slabs(hbuf.at[slot], x_ref[...] * (1.0 + sc_ref[...]) + sh_ref[...])

    def issue(rb, _):
        for u in range(SUBLANES):
            r = rb * SUBLANES + u
            for k in range(TOP_K):
                _row_copy(hbuf.at[slot], r, buf_ref, pos_ref[r * TOP_K + k], sem.at[slot]).start(priority=k)
        return 0

    lax.fori_loop(0, tm // SUBLANES, issue, 0)

    def drain(s):
        for k in range(TOP_K):
            pltpu.make_async_copy(hbuf.at[s], buf_ref.at[pl.ds(0, tm * SLAB), :], sem.at[s]).wait()

    @pl.when(i > 0)
    def _():
        drain(1 - slot)

    @pl.when(i == nt - 1)
    def _():
        drain(slot)


def _flat_positions(pos):
    return pos[:TOP_K].T.reshape(-1)


def _dispatch(x, sc, sh, pos, pend, n_rows, seq, tm=512):
    t, d = x.shape
    tpb = seq // tm
    nt = t // tm
    pos3 = _flat_positions(pos)
    vec = pl.BlockSpec((None, 1, d), lambda i, p: (i // tpb, 0, 0))
    return pl.pallas_call(
        _dispatch_kernel,
        grid_spec=pltpu.PrefetchScalarGridSpec(
            num_scalar_prefetch=1,
            grid=(nt,),
            in_specs=[pl.BlockSpec((tm, d), lambda i, p: (i, 0)), vec, vec,
                      pl.BlockSpec((TOP_K * tm,), lambda i, p: (i,), memory_space=pltpu.SMEM)],
            out_specs=pl.BlockSpec(memory_space=pl.ANY),
            scratch_shapes=[pltpu.VMEM((2, tm * SLAB, LANES), F32), pltpu.VMEM((MOE_BM * SLAB, LANES), F32),
                            pltpu.SemaphoreType.DMA((2,)), pltpu.SemaphoreType.DMA(())],
        ),
        out_shape=jax.ShapeDtypeStruct((n_rows * SLAB, LANES), F32),
        compiler_params=_cparams(1),
        name="moe_dispatch",
    )(pend, x, sc, sh, pos3)


def _experts_kernel(layer, blk_e_ref, nb_ref, first_ref, slot_ref, next_ref, x_ref, w1_ref, w3_ref, w2_ref,
                    y_ref, wf1, wf3, wf2, wb1, wb3, wb2, sem):
    j = pl.program_id(0)
    slot = slot_ref[j]

    def weight_copies(e, s):
        return [pltpu.make_async_copy(w_ref.at[layer, e], wf.at[s], sem.at[s])
                for w_ref, wf in ((w1_ref, wf1), (w3_ref, wf3), (w2_ref, wf2))]

    @pl.when(j == 0)
    def _():
        for cp in weight_copies(blk_e_ref[0], slot):
            cp.start()

    @pl.when(first_ref[j] == 1)
    def _():
        for cp in weight_copies(blk_e_ref[j], slot):
            cp.wait()
        wb1[...] = wf1[slot].astype(BF16)
        wb3[...] = wf3[slot].astype(BF16)
        wb2[...] = wf2[slot].astype(BF16)

        @pl.when(next_ref[j] >= 0)
        def _():
            for cp in weight_copies(next_ref[j], 1 - slot):
                cp.start()

    @pl.when(j < nb_ref[0])
    def _():
        xb = _from_slabs(x_ref, MOE_BM).astype(BF16)
        h1 = jnp.dot(xb, wb1[...], preferred_element_type=F32)
        h3 = jnp.dot(xb, wb3[...], preferred_element_type=F32)
        a = (_silu(h1) * h3).astype(BF16)
        _to_slabs(y_ref, jnp.dot(a, wb2[...], preferred_element_type=F32))

    @pl.when(j >= nb_ref[0])
    def _():
        y_ref[...] = jnp.zeros_like(y_ref)


def _experts(buf, blk_e, nb_used, w1, w3, w2, layer):
    nblk = buf.shape[0] // (MOE_BM * SLAB)
    d, de = w1.shape[2], w1.shape[3]
    idx = jnp.arange(nblk, dtype=I32)
    first = jnp.concatenate([jnp.ones((1,), I32), (blk_e[1:] != blk_e[:-1]).astype(I32)])
    slot = (jnp.cumsum(first) - 1) % 2
    later_first = lax.cummin(jnp.where(first == 1, idx, nblk)[::-1])[::-1]
    next_first = jnp.concatenate([later_first[1:], jnp.full((1,), nblk, I32)])
    next_e = jnp.where(next_first < nblk, blk_e[jnp.minimum(next_first, nblk - 1)], -1).astype(I32)
    row_blk = lambda j, be, nb, *runs: (jnp.minimum(j, nb[0] - 1), 0)
    hbm = pl.BlockSpec(memory_space=pl.ANY)
    return pl.pallas_call(
        functools.partial(_experts_kernel, layer),
        grid_spec=pltpu.PrefetchScalarGridSpec(
            num_scalar_prefetch=5,
            grid=(nblk,),
            in_specs=[pl.BlockSpec((MOE_BM * SLAB, LANES), row_blk), hbm, hbm, hbm],
            out_specs=pl.BlockSpec((MOE_BM * SLAB, LANES), lambda j, *_: (j, 0)),
            scratch_shapes=[pltpu.VMEM((2, d, de), F32), pltpu.VMEM((2, d, de), F32), pltpu.VMEM((2, de, d), F32),
                            pltpu.VMEM((d, de), BF16), pltpu.VMEM((d, de), BF16), pltpu.VMEM((de, d), BF16),
                            pltpu.SemaphoreType.DMA((2,))],
        ),
        out_shape=jax.ShapeDtypeStruct(buf.shape, F32),
        compiler_params=_cparams(1),
        name="moe_experts",
    )(blk_e, nb_used, first, slot.astype(I32), next_e, buf, w1, w3, w2)


def _combine_kernel(pos_ref, posn_ref, y_ref, wcol_ref, x_ref, g_ref, lng_ref, lnb_ref, o_ref, ybuf, sem):
    tm = x_ref.shape[0]
    i = pl.program_id(0)
    nt = pl.num_programs(0)
    slot = i % 2

    def gather(p_ref, s):
        def issue(rb, _):
            for u in range(SUBLANES):
                r = rb * SUBLANES + u
                for k in range(TOP_K):
                    _row_copy(y_ref, p_ref[r * TOP_K + k], ybuf.at[s, k], r, sem.at[s]).start(priority=k)
            return 0

        lax.fori_loop(0, tm // SUBLANES, issue, 0)

    @pl.when(i == 0)
    def _():
        gather(pos_ref, 0)

    @pl.when(i + 1 < nt)
    def _():
        gather(posn_ref, 1 - slot)

    for k in range(TOP_K):
        pltpu.make_async_copy(y_ref.at[pl.ds(0, tm * SLAB), :], ybuf.at[slot, k], sem.at[slot]).wait()
    y = (wcol_ref[:, 0:1] * _from_slabs(ybuf.at[slot, 0], tm)
         + wcol_ref[:, 1:2] * _from_slabs(ybuf.at[slot, 1], tm))
    o_ref[...] = _layer_norm(DN_ALPHA * x_ref[...] + g_ref[...] * y, lng_ref[...], lnb_ref[...])


def _combine(y, pos, wcol, x, gate, lng, lnb, seq, tm=512):
    t, d = x.shape
    tpb = seq // tm
    nt = t // tm
    pos3 = _flat_positions(pos)
    row = pl.BlockSpec((1, d), lambda i: (0, 0))
    return pl.pallas_call(
        _combine_kernel,
        grid=(nt,),
        in_specs=[pl.BlockSpec((TOP_K * tm,), lambda i: (i,), memory_space=pltpu.SMEM),
                  pl.BlockSpec((TOP_K * tm,), lambda i: (jnp.minimum(i + 1, nt - 1),), memory_space=pltpu.SMEM),
                  pl.BlockSpec(memory_space=pl.ANY),
                  pl.BlockSpec((tm, LANES), lambda i: (i, 0)),
                  pl.BlockSpec((tm, d), lambda i: (i, 0)),
                  pl.BlockSpec((None, 1, d), lambda i: (i // tpb, 0, 0)),
                  row, row],
        out_specs=pl.BlockSpec((tm, d), lambda i: (i, 0)),
        out_shape=jax.ShapeDtypeStruct((t, d), F32),
        scratch_shapes=[pltpu.VMEM((2, TOP_K, tm * SLAB, LANES), F32), pltpu.SemaphoreType.DMA((2,))],
        compiler_params=_cparams(1),
        name="moe_combine",
    )(pos3, pos3, y, wcol, x, gate, lng.reshape(1, d), lnb.reshape(1, d))


def _moe_layer(x, mod, w_group, b_group, w_router, b_router, w1, w3, w2, layer, lng, lnb, seq):
    sh, sc, gate = mod
    t, d = x.shape
    e, rank, wcol, cnt = _router(x, sc, sh, w_group, b_group, w_router, b_router, seq)
    counts = cnt[:, 0].astype(I32)
    padded = (counts + MOE_BM - 1) // MOE_BM * MOE_BM
    pend = jnp.cumsum(padded)
    pstart = pend - padded
    n_rows = t * TOP_K + N_EXPERTS * MOE_BM
    nblk = n_rows // MOE_BM
    blk_start = jnp.arange(nblk, dtype=I32) * MOE_BM
    blk_e = jnp.minimum(jnp.sum(pend[None, :] <= blk_start[:, None], axis=1), N_EXPERTS - 1).astype(I32)
    nb_used = (pend[-1:] // MOE_BM).astype(I32)
    pos = _positions(e, rank, pstart)
    buf = _dispatch(x, sc, sh, pos, pend.astype(I32), n_rows, seq)
    y = _experts(buf, blk_e, nb_used, w1, w3, w2, layer)
    return _combine(y, pos, wcol, x, gate, lng, lnb, seq)


def kernel(x, c, ada_w, ada_b, ln1_g, ln1_b, ln2_g, ln2_b, mlstm_w_in, mlstm_conv_w, mlstm_conv_b, mlstm_b_if, mlstm_norm_g, mlstm_w_out, s5_w_in, s5_lam_re, s5_lam_im, s5_log_dt, s5_b_re, s5_b_im, s5_c_re, s5_c_im, s5_d, s5_w_glu, swa_w_qkv, swa_b_qkv, swa_sinks, swa_w_o, moe_w_group, moe_b_group, moe_w_router, moe_b_router, moe_w1, moe_w3, moe_w2):
    bn, seq, d = x.shape
    depth = ada_w.shape[0]
    mods = _ada(c, ada_w, ada_b).reshape(depth, bn, 6, 1, d)
    xt = x.reshape(bn * seq, d)
    for i in range(depth):
        m1 = tuple(mods[i, :, k] for k in range(3))
        m2 = tuple(mods[i, :, k] for k in range(3, 6))
        kind, j = i % N_MIXERS, i // N_MIXERS
        if kind == 0:
            xt = _mlstm_layer(xt, m1, mlstm_w_in[j], mlstm_conv_w[j], mlstm_conv_b[j], mlstm_b_if[j],
                              mlstm_norm_g[j], mlstm_w_out[j], ln1_g[i], ln1_b[i], bn, seq)
        elif kind == 1:
            xt = _s5_layer(xt, m1, s5_w_in[j], s5_lam_re[j], s5_lam_im[j], s5_log_dt[j], s5_b_re[j], s5_b_im[j],
                           s5_c_re[j], s5_c_im[j], s5_d[j], s5_w_glu[j], ln1_g[i], ln1_b[i], bn, seq)
        else:
            xt = _swa_layer(xt, m1, swa_w_qkv[j], swa_b_qkv[j], swa_sinks[j], swa_w_o[j],
                            ln1_g[i], ln1_b[i], bn, seq)
        xt = _moe_layer(xt, m2, moe_w_group[i], moe_b_group[i], moe_w_router[i], moe_b_router[i],
                        moe_w1, moe_w3, moe_w2, i, ln2_g[i], ln2_b[i], seq)
    return xt.reshape(bn, seq, d)
```
